```python
import jax, jax.numpy as jnp
from jax import lax
import numpy as np

D_MODEL = 2048
BATCH = 8
SEQ = 4096
DEPTH = 4

CHUNK = 64
Q_BLOCK = 128
N_MIXERS = 3
D_FF = 5632
FFN_RESIDUAL_WEIGHT = 0.5
RMS_EPS = 1e-6

A_HEADS = 8
A_QK_DIM = D_MODEL // (2 * A_HEADS)
A_V_DIM = D_MODEL // A_HEADS
A_IN = 2 * A_HEADS * A_QK_DIM + 2 * A_HEADS * A_V_DIM + 2 * A_HEADS
B_HEADS = 16
B_HEAD_DIM = D_MODEL // B_HEADS
B_IN = 3 * B_HEADS * B_HEAD_DIM + B_HEADS
C_HEADS = 4
C_K_DIM = D_MODEL // (2 * C_HEADS)
C_V_DIM = D_MODEL // C_HEADS
C_GATE_RANK = 16
C_GATE_TEMP = 16.0
C_IN = 2 * C_HEADS * C_K_DIM + 2 * C_HEADS * C_V_DIM + C_GATE_RANK

N_LAYERS_A = len(range(0, DEPTH, N_MIXERS))
N_LAYERS_B = len(range(1, DEPTH, N_MIXERS))
N_LAYERS_C = len(range(2, DEPTH, N_MIXERS))

kernel_name = "hybrid_mlstm_fox_gla_macaron_trunk"


def rmsnorm(x, g):
    xf = x.astype(jnp.float32)
    y = xf * lax.rsqrt(jnp.mean(xf * xf, axis=-1, keepdims=True) + RMS_EPS)
    return (y * g.astype(jnp.float32)).astype(x.dtype)


def head_rmsnorm(h, g):
    y = h * lax.rsqrt(jnp.mean(h * h, axis=-1, keepdims=True) + RMS_EPS)
    return y.reshape(h.shape[0], h.shape[1], -1) * g.astype(jnp.float32)


def swiglu_ffn(x, g, w_in, w_out):
    a, b = jnp.split(rmsnorm(x, g) @ w_in, 2, axis=-1)
    return (jax.nn.silu(a) * b) @ w_out


def to_chunks(t):
    b, s, h = t.shape[:3]
    t = t.reshape(b, s // CHUNK, CHUNK, h, *t.shape[3:])
    return jnp.moveaxis(t, (1, 3), (0, 2))


def from_chunks(t):
    t = jnp.moveaxis(t, (0, 2), (1, 3))
    return t.reshape(t.shape[0], t.shape[1] * t.shape[2], t.shape[3], t.shape[4])


def mlstm_chunk_step(carry, inp):
    c_prev, n_prev, m_prev = carry
    q, k, v, i_pre, log_f = inp
    tri = jnp.tril(jnp.ones((CHUNK, CHUNK), dtype=bool))
    b = jnp.cumsum(log_f, axis=-1)
    d_log = jnp.where(tri, b[..., :, None] - b[..., None, :] + i_pre[..., None, :], -jnp.inf)
    inter_log = b + m_prev[..., None]
    m_t = jnp.maximum(inter_log, jnp.max(d_log, axis=-1))
    d_w = jnp.exp(d_log - m_t[..., None])
    inter_w = jnp.exp(inter_log - m_t)
    s = jnp.einsum('bhtd,bhsd->bhts', q, k) * d_w
    num = jnp.einsum('bhts,bhse->bhte', s, v) + inter_w[..., None] * jnp.einsum('bhtd,bhde->bhte', q, c_prev)
    den = jnp.sum(s, axis=-1) + inter_w * jnp.einsum('bhtd,bhd->bht', q, n_prev)
    h = num / jnp.maximum(jnp.abs(den), jnp.exp(-m_t))[..., None]
    b_last = b[..., -1]
    g = b_last[..., None] - b + i_pre
    m_new = jnp.maximum(b_last + m_prev, jnp.max(g, axis=-1))
    w_c = jnp.exp(b_last + m_prev - m_new)
    w_s = jnp.exp(g - m_new[..., None])
    c_new = w_c[..., None, None] * c_prev + jnp.einsum('bhs,bhsd,bhse->bhde', w_s, k, v)
    n_new = w_c[..., None] * n_prev + jnp.einsum('bhs,bhsd->bhd', w_s, k)
    return (c_new, n_new, m_new), h


def mlstm_mixer(u, w_in, b_if, head_norm_g, w_out):
    bsz, s, _ = u.shape
    nq, nv = A_HEADS * A_QK_DIM, A_HEADS * A_V_DIM
    p = (u @ w_in).astype(jnp.float32)
    q, k, v, o, gi, gf = jnp.split(p, [nq, 2 * nq, 2 * nq + nv, 2 * nq + 2 * nv, 2 * nq + 2 * nv + A_HEADS], axis=-1)
    b_if = b_if.astype(jnp.float32)
    i_pre = gi + b_if[:A_HEADS]
    log_f = jax.nn.log_sigmoid(gf + b_if[A_HEADS:])
    q = q.reshape(bsz, s, A_HEADS, A_QK_DIM)
    k = k.reshape(bsz, s, A_HEADS, A_QK_DIM) * (A_QK_DIM ** -0.5)
    v = v.reshape(bsz, s, A_HEADS, A_V_DIM)
    init = (jnp.zeros((bsz, A_HEADS, A_QK_DIM, A_V_DIM), jnp.float32),
            jnp.zeros((bsz, A_HEADS, A_QK_DIM), jnp.float32),
            jnp.zeros((bsz, A_HEADS), jnp.float32))
    _, h = lax.scan(mlstm_chunk_step, init,
                    (to_chunks(q), to_chunks(k), to_chunks(v), to_chunks(i_pre), to_chunks(log_f)))
    h = head_rmsnorm(from_chunks(h), head_norm_g) * jax.nn.sigmoid(o)
    return h.astype(u.dtype) @ w_out


def fox_mixer(u, w_in, b_f, w_out):
    bsz, s, _ = u.shape
    nd = B_HEADS * B_HEAD_DIM
    q, k, v, gf = jnp.split(u @ w_in, [nd, 2 * nd, 3 * nd], axis=-1)
    q = q.reshape(bsz, s, B_HEADS, B_HEAD_DIM).transpose(0, 2, 1, 3)
    k = k.reshape(bsz, s, B_HEADS, B_HEAD_DIM).transpose(0, 2, 1, 3)
    v = v.reshape(bsz, s, B_HEADS, B_HEAD_DIM).transpose(0, 2, 1, 3)
    log_f = jax.nn.log_sigmoid(gf.astype(jnp.float32) + b_f.astype(jnp.float32))
    cum_f = jnp.cumsum(log_f, axis=1).transpose(0, 2, 1)
    scale = B_HEAD_DIM ** -0.5
    outs = []
    for blk in range(s // Q_BLOCK):
        lo, hi = blk * Q_BLOCK, (blk + 1) * Q_BLOCK
        logits = jnp.einsum('bhtd,bhsd->bhts', q[:, :, lo:hi], k[:, :, :hi],
                            preferred_element_type=jnp.float32) * scale
        logits = logits + cum_f[:, :, lo:hi, None] - cum_f[:, :, None, :hi]
        mask = (lo + jnp.arange(Q_BLOCK))[:, None] >= jnp.arange(hi)[None, :]
        probs = jax.nn.softmax(jnp.where(mask, logits, -jnp.inf), axis=-1)
        outs.append(jnp.einsum('bhts,bhsd->bhtd', probs.astype(v.dtype), v[:, :, :hi]))
    o = jnp.concatenate(outs, axis=2).transpose(0, 2, 1, 3).reshape(bsz, s, nd)
    return o @ w_out


def gla_chunk_step(state, inp):
    q, k, v, la = inp
    tri = jnp.tril(jnp.ones((CHUNK, CHUNK), dtype=bool))
    b = jnp.cumsum(la, axis=2)
    diff = b[:, :, :, None, :] - b[:, :, None, :, :]
    decay = jnp.exp(jnp.where(tri[:, :, None], diff, -jnp.inf))
    a = jnp.einsum('bhtd,bhsd,bhtsd->bhts', q, k, decay)
    o = jnp.einsum('bhts,bhse->bhte', a, v) + jnp.einsum('bhtd,bhde->bhte', q * jnp.exp(b), state)
    b_last = b[:, :, -1]
    state = jnp.exp(b_last)[..., None] * state + jnp.einsum(
        'bhsd,bhse->bhde', k * jnp.exp(b_last[:, :, None] - b), v)
    return state, o


def gla_mixer(u, w_in, w_alpha, b_alpha, head_norm_g, w_out):
    bsz, s, _ = u.shape
    nk, nv = C_HEADS * C_K_DIM, C_HEADS * C_V_DIM
    p = (u @ w_in).astype(jnp.float32)
    q, k, v, r, a_low = jnp.split(p, [nk, 2 * nk, 2 * nk + nv, 2 * nk + 2 * nv], axis=-1)
    log_alpha = jax.nn.log_sigmoid(a_low @ w_alpha.astype(jnp.float32) + b_alpha.astype(jnp.float32)) / C_GATE_TEMP
    q = q.reshape(bsz, s, C_HEADS, C_K_DIM) * (C_K_DIM ** -0.5)
    k = k.reshape(bsz, s, C_HEADS, C_K_DIM)
    v = v.reshape(bsz, s, C_HEADS, C_V_DIM)
    log_alpha = log_alpha.reshape(bsz, s, C_HEADS, C_K_DIM)
    init = jnp.zeros((bsz, C_HEADS, C_K_DIM, C_V_DIM), jnp.float32)
    _, o = lax.scan(gla_chunk_step, init,
                    (to_chunks(q), to_chunks(k), to_chunks(v), to_chunks(log_alpha)))
    h = head_rmsnorm(from_chunks(o), head_norm_g) * jax.nn.silu(r)
    return h.astype(u.dtype) @ w_out


def setup_inputs(seed: int = 0) -> dict:
    key = jax.random.key(seed)
    ks = jax.random.split(key, 24)
    f32 = jnp.float32
    nrm = lambda k, shape, fan_in: jax.random.normal(k, shape, f32) * (fan_in ** -0.5)
    gain = lambda k, shape: 1.0 + 0.02 * jax.random.normal(k, shape, f32)
    x = jax.random.normal(ks[0], (BATCH, SEQ, D_MODEL), f32)
    ffn_norm = gain(ks[1], (DEPTH, 2, D_MODEL))
    ffn_w_in = nrm(ks[2], (DEPTH, 2, D_MODEL, 2 * D_FF), D_MODEL)
    ffn_w_out = nrm(ks[3], (DEPTH, 2, D_FF, D_MODEL), D_FF)
    mix_norm = gain(ks[4], (DEPTH, D_MODEL))
    a_w_in = nrm(ks[5], (N_LAYERS_A, D_MODEL, A_IN), D_MODEL)
    a_b_if = jnp.concatenate([
        -2.0 + 0.1 * jax.random.normal(ks[6], (N_LAYERS_A, A_HEADS), f32),
        3.0 + 0.1 * jax.random.normal(ks[7], (N_LAYERS_A, A_HEADS), f32)], axis=-1)
    a_head_norm = gain(ks[8], (N_LAYERS_A, A_HEADS * A_V_DIM))
    a_w_out = nrm(ks[9], (N_LAYERS_A, A_HEADS * A_V_DIM, D_MODEL), A_HEADS * A_V_DIM)
    b_w_in = nrm(ks[10], (N_LAYERS_B, D_MODEL, B_IN), D_MODEL)
    b_b_f = 2.0 + 0.1 * jax.random.normal(ks[11], (N_LAYERS_B, B_HEADS), f32)
    b_w_out = nrm(ks[12], (N_LAYERS_B, B_HEADS * B_HEAD_DIM, D_MODEL), B_HEADS * B_HEAD_DIM)
    c_w_in = nrm(ks[13], (N_LAYERS_C, D_MODEL, C_IN), D_MODEL)
    c_w_alpha = nrm(ks[14], (N_LAYERS_C, C_GATE_RANK, C_HEADS * C_K_DIM), C_GATE_RANK)
    c_b_alpha = 0.1 * jax.random.normal(ks[15], (N_LAYERS_C, C_HEADS * C_K_DIM), f32)
    c_head_norm = gain(ks[16], (N_LAYERS_C, C_HEADS * C_V_DIM))
    c_w_out = nrm(ks[17], (N_LAYERS_C, C_HEADS * C_V_DIM, D_MODEL), C_HEADS * C_V_DIM)
    final_norm = gain(ks[18], (D_MODEL,))
    return {"x": x, "ffn_norm": ffn_norm, "ffn_w_in": ffn_w_in, "ffn_w_out": ffn_w_out,
            "mix_norm": mix_norm, "a_w_in": a_w_in, "a_b_if": a_b_if, "a_head_norm": a_head_norm,
            "a_w_out": a_w_out, "b_w_in": b_w_in, "b_b_f": b_b_f, "b_w_out": b_w_out,
            "c_w_in": c_w_in, "c_w_alpha": c_w_alpha, "c_b_alpha": c_b_alpha,
            "c_head_norm": c_head_norm, "c_w_out": c_w_out, "final_norm": final_norm}


def reference(x, ffn_norm, ffn_w_in, ffn_w_out, mix_norm, a_w_in, a_b_if, a_head_norm, a_w_out,
              b_w_in, b_b_f, b_w_out, c_w_in, c_w_alpha, c_b_alpha, c_head_norm, c_w_out, final_norm):
    h = x
    for i in range(DEPTH):
        h = h + FFN_RESIDUAL_WEIGHT * swiglu_ffn(h, ffn_norm[i, 0], ffn_w_in[i, 0], ffn_w_out[i, 0])
        u = rmsnorm(h, mix_norm[i])
        kind, j = i % N_MIXERS, i // N_MIXERS
        if kind == 0:
            y = mlstm_mixer(u, a_w_in[j], a_b_if[j], a_head_norm[j], a_w_out[j])
        elif kind == 1:
            y = fox_mixer(u, b_w_in[j], b_b_f[j], b_w_out[j])
        else:
            y = gla_mixer(u, c_w_in[j], c_w_alpha[j], c_b_alpha[j], c_head_norm[j], c_w_out[j])
        h = h + y
        h = h + FFN_RESIDUAL_WEIGHT * swiglu_ffn(h, ffn_norm[i, 1], ffn_w_in[i, 1], ffn_w_out[i, 1])
    return rmsnorm(h, final_norm)
```

```python
import functools

import jax
import jax.numpy as jnp
from jax import lax
from jax.experimental import pallas as pl
from jax.experimental.pallas import tpu as pltpu

F32 = jnp.float32
BF16 = jnp.bfloat16

RMS_EPS = 1e-6
FFN_RESIDUAL_WEIGHT = 0.5
GLA_GATE_TEMP = 16.0
NEG_BIG = -1e30

LANES = 128
MLSTM_QK, MLSTM_V = 128, 256
FOX_HEAD = 128
GLA_K, GLA_V = 256, 512

VMEM_LIMIT_BYTES = 60 * 1024 * 1024

FFN_TM, FFN_TF, FFN_ROWS = 1024, 512, 256
PROJ_TM, PROJ_TN = 1024, 1024
OUT_TM = 512
MLSTM_CHUNK = 256
FOX_TQ = 512
FOX_GATE_CHUNK = 512
GLA_CHUNK = 64


def _params(*sem):
    return pltpu.CompilerParams(dimension_semantics=sem, vmem_limit_bytes=VMEM_LIMIT_BYTES)


def _tile(n, pref):
    if n <= pref:
        return n
    return max(c for c in range(LANES, pref + 1, LANES) if n % c == 0)


def _dot(a, b):
    return jnp.dot(a, b, preferred_element_type=F32)


def _dot_nt(a, b):
    return lax.dot_general(a, b, (((1,), (1,)), ((), ())), preferred_element_type=F32)


def _dot_tn(a, b):
    return lax.dot_general(a, b, (((0,), (0,)), ((), ())), preferred_element_type=F32)


def _split2(x):
    hi = x.astype(BF16)
    lo = (x - hi.astype(F32)).astype(BF16)
    return hi, lo


def _split3(x):
    hi = x.astype(BF16)
    r = x - hi.astype(F32)
    mid = r.astype(BF16)
    lo = (r - mid.astype(F32)).astype(BF16)
    return hi, mid, lo


def _dot_f32(a, b):
    ah, al = _split2(a)
    bh, bl = _split2(b)
    return _dot(ah, bh) + (_dot(ah, bl) + _dot(al, bh))


def _cumsum_rows(x):
    n = x.shape[0]
    tril = (lax.broadcasted_iota(jnp.int32, (n, n), 0) >= lax.broadcasted_iota(jnp.int32, (n, n), 1)).astype(BF16)
    hi, mid, lo = _split3(x)
    return _dot(tril, hi) + (_dot(tril, mid) + _dot(tril, lo))


def _cumsum_lanes(x):
    n = x.shape[1]
    triu = (lax.broadcasted_iota(jnp.int32, (n, n), 0) <= lax.broadcasted_iota(jnp.int32, (n, n), 1)).astype(BF16)
    hi, mid, lo = _split3(x)
    return _dot(hi, triu) + (_dot(mid, triu) + _dot(lo, triu))


def _rmsnorm(x, g):
    return x * lax.rsqrt(jnp.mean(x * x, axis=-1, keepdims=True) + RMS_EPS) * g


def _ffn_body(x_ref, g_ref, wa_ref, wb_ref, wo_ref, fg_ref, o_ref, u_ref, *, final_norm):
    f = pl.program_id(1)
    tm = x_ref.shape[0]
    rs = min(FFN_ROWS, tm)
    row_tiles = [pl.ds(r * rs, rs) for r in range(tm // rs)]

    @pl.when(f == 0)
    def _():
        for rows in row_tiles:
            u_ref[rows, :] = _rmsnorm(x_ref[rows, :], g_ref[...]).astype(BF16)
            o_ref[rows, :] = jnp.zeros((rs, o_ref.shape[1]), F32)

    for rows in row_tiles:
        u = u_ref[rows, :]
        a = _dot(u, wa_ref[...])
        b = _dot(u, wb_ref[...])
        hid = (a * jax.nn.sigmoid(a) * b).astype(BF16)
        o_ref[rows, :] += _dot(hid, wo_ref[...])

    @pl.when(f == pl.num_programs(1) - 1)
    def _():
        for rows in row_tiles:
            r = x_ref[rows, :] + FFN_RESIDUAL_WEIGHT * o_ref[rows, :]
            if final_norm:
                r = _rmsnorm(r, fg_ref[...])
            o_ref[rows, :] = r


def _ffn(h, g, w_in, w_out, final_g=None):
    t, d = h.shape
    dff = w_out.shape[0]
    tm, tf = _tile(t, FFN_TM), _tile(dff, FFN_TF)
    assert t % tm == 0 and dff % tf == 0
    nf = dff // tf
    fg = g if final_g is None else final_g
    return pl.pallas_call(
        functools.partial(_ffn_body, final_norm=final_g is not None),
        grid=(t // tm, nf),
        in_specs=[
            pl.BlockSpec((tm, d), lambda i, f: (i, 0)),
            pl.BlockSpec((1, d), lambda i, f: (0, 0)),
            pl.BlockSpec((d, tf), lambda i, f: (0, f)),
            pl.BlockSpec((d, tf), lambda i, f: (0, f + nf)),
            pl.BlockSpec((tf, d), lambda i, f: (f, 0)),
            pl.BlockSpec((1, d), lambda i, f: (0, 0)),
        ],
        out_specs=pl.BlockSpec((tm, d), lambda i, f: (i, 0)),
        out_shape=jax.ShapeDtypeStruct((t, d), F32),
        scratch_shapes=[pltpu.VMEM((tm, d), BF16)],
        compiler_params=_params("parallel", "arbitrary"),
        name="ffn",
    )(h, g.reshape(1, d), w_in, w_in, w_out, fg.reshape(1, d))


def _proj_body(x_ref, g_ref, w_ref, wg_ref, p_ref, gate_ref, u_ref):
    @pl.when(pl.program_id(1) == 0)
    def _():
        tm = x_ref.shape[0]
        rs = min(FFN_ROWS, tm)
        for r in range(tm // rs):
            rows = pl.ds(r * rs, rs)
            u = _rmsnorm(x_ref[rows, :], g_ref[...])
            u_ref[rows, :] = u.astype(BF16)
            gate_ref[rows, :] = _dot_f32(u, wg_ref[...])

    p_ref[...] = _dot(u_ref[...], w_ref[...]).astype(BF16)


def _proj(h, g, w, wg):
    t, d = h.shape
    n, gw = w.shape[1], wg.shape[1]
    tm, tn = _tile(t, PROJ_TM), _tile(n, PROJ_TN)
    assert t % tm == 0 and n % tn == 0
    return pl.pallas_call(
        _proj_body,
        grid=(t // tm, n // tn),
        in_specs=[
            pl.BlockSpec((tm, d), lambda i, j: (i, 0)),
            pl.BlockSpec((1, d), lambda i, j: (0, 0)),
            pl.BlockSpec((d, tn), lambda i, j: (0, j)),
            pl.BlockSpec((d, gw), lambda i, j: (0, 0)),
        ],
        out_specs=[
            pl.BlockSpec((tm, tn), lambda i, j: (i, j)),
            pl.BlockSpec((tm, gw), lambda i, j: (i, 0)),
        ],
        out_shape=[jax.ShapeDtypeStruct((t, n), BF16), jax.ShapeDtypeStruct((t, gw), F32)],
        scratch_shapes=[pltpu.VMEM((tm, d), BF16)],
        compiler_params=_params("parallel", "arbitrary"),
        name="mixer_in_proj",
    )(h, g.reshape(1, d), w, wg)


def _out_body(h_ref, y_ref, w_ref, o_ref):
    o_ref[...] = h_ref[...] + _dot(y_ref[...], w_ref[...])


def _out_proj(h, y, w):
    t, d = h.shape
    k = y.shape[1]
    tm = min(OUT_TM, t)
    assert t % tm == 0
    return pl.pallas_call(
        _out_body,
        grid=(t // tm,),
        in_specs=[
            pl.BlockSpec((tm, d), lambda i: (i, 0)),
            pl.BlockSpec((tm, k), lambda i: (i, 0)),
            pl.BlockSpec((k, d), lambda i: (0, 0)),
        ],
        out_specs=pl.BlockSpec((tm, d), lambda i: (i, 0)),
        out_shape=jax.ShapeDtypeStruct((t, d), F32),
        compiler_params=_params("parallel"),
        name="mixer_out_proj",
    )(h, y, w)


def _mlstm_body(q_ref, k_ref, v_ref, o_ref, gate_ref, bias_ref, hn_ref, y_ref, c_ref, m_ref, *, heads):
    L = q_ref.shape[1]
    dqk, dv = MLSTM_QK, MLSTM_V
    scale = dqk ** -0.5

    @pl.when(pl.program_id(1) == 0)
    def _():
        c_ref[...] = jnp.zeros_like(c_ref)
        m_ref[...] = jnp.zeros_like(m_ref)

    gates = gate_ref[0] + bias_ref[...]
    i_pre = gates[:, :LANES]
    b_cum = _cumsum_rows(jax.nn.log_sigmoid(gates[:, LANES:]))
    b_t = b_cum.T
    i_t = i_pre.T
    m_all = m_ref[...]
    row = lax.broadcasted_iota(jnp.int32, (L, L), 0)
    col = lax.broadcasted_iota(jnp.int32, (L, L), 1)
    causal = col <= row
    ones_blk = jnp.ones((L, LANES), BF16)

    for h in range(heads):
        bcol, icol = b_cum[:, h:h + 1], i_pre[:, h:h + 1]
        brow, irow = b_t[h:h + 1, :], i_t[h:h + 1, :]
        m_prev = m_all[:, h:h + 1]
        d_log = jnp.where(causal, bcol - brow + irow, NEG_BIG)
        inter_log = bcol + m_prev
        m_t = jnp.maximum(inter_log, jnp.max(d_log, axis=1, keepdims=True))
        d_w = jnp.exp(d_log - m_t)
        inter_w = jnp.exp(inter_log - m_t)

        q = q_ref[0, :, h * dqk:(h + 1) * dqk]
        k = k_ref[0, :, h * dqk:(h + 1) * dqk]
        v_ext = jnp.concatenate([v_ref[0, :, h * dv:(h + 1) * dv], ones_blk], axis=1)
        s = _dot_nt(q, k) * (d_w * scale)
        c_prev = c_ref[h]
        nd = _dot(s.astype(BF16), v_ext) + inter_w * _dot(q, c_prev.astype(BF16))
        num, den = nd[:, :dv], nd[:, dv:dv + 1]
        hh = num / jnp.maximum(jnp.abs(den), jnp.exp(-m_t))
        hh = _rmsnorm(hh, hn_ref[:, h * dv:(h + 1) * dv])
        gate_o = jax.nn.sigmoid(o_ref[0, :, h * dv:(h + 1) * dv].astype(F32))
        y_ref[0, :, h * dv:(h + 1) * dv] = (hh * gate_o).astype(BF16)

        b_last = bcol[L - 1:L, :]
        g = b_last - bcol + icol
        m_new = jnp.maximum(b_last + m_prev, jnp.max(g, axis=0, keepdims=True))
        w_c = jnp.exp(b_last + m_prev - m_new)
        w_s = jnp.exp(g - m_new)
        k_w = (k.astype(F32) * (w_s * scale)).astype(BF16)
        c_ref[h] = w_c * c_prev + _dot_tn(k_w, v_ext)
        m_ref[:, h:h + 1] = m_new


def _mlstm_core(p, gates, bias, hn, heads):
    bsz, s, _ = p.shape
    nq, nv = heads * MLSTM_QK, heads * MLSTM_V
    L = min(MLSTM_CHUNK, s)
    assert s % L == 0 and nv == 2 * nq
    return pl.pallas_call(
        functools.partial(_mlstm_body, heads=heads),
        grid=(bsz, s // L),
        in_specs=[
            pl.BlockSpec((1, L, nq), lambda b, c: (b, c, 0)),
            pl.BlockSpec((1, L, nq), lambda b, c: (b, c, 1)),
            pl.BlockSpec((1, L, nv), lambda b, c: (b, c, 1)),
            pl.BlockSpec((1, L, nv), lambda b, c: (b, c, 2)),
            pl.BlockSpec((1, L, 2 * LANES), lambda b, c: (b, c, 0)),
            pl.BlockSpec((1, 2 * LANES), lambda b, c: (0, 0)),
            pl.BlockSpec((1, nv), lambda b, c: (0, 0)),
        ],
        out_specs=pl.BlockSpec((1, L, nv), lambda b, c: (b, c, 0)),
        out_shape=jax.ShapeDtypeStruct((bsz, s, nv), BF16),
        scratch_shapes=[pltpu.VMEM((heads, MLSTM_QK, MLSTM_V + LANES), F32), pltpu.VMEM((1, LANES), F32)],
        compiler_params=_params("parallel", "arbitrary"),
        name="mlstm_core",
    )(p, p, p, p, gates, bias, hn)


def _fox_gate_body(gate_ref, bias_ref, cum_ref, carry_ref):
    @pl.when(pl.program_id(1) == 0)
    def _():
        carry_ref[...] = jnp.zeros_like(carry_ref)

    log_f = jax.nn.log_sigmoid(gate_ref[0] + bias_ref[...])
    cum = _cumsum_lanes(log_f.T) + carry_ref[...]
    cum_ref[0] = cum
    carry_ref[...] = cum[:, cum.shape[1] - 1:]


def _fox_gates(gates, bias):
    bsz, s, _ = gates.shape
    tc = min(FOX_GATE_CHUNK, s)
    assert s % tc == 0
    return pl.pallas_call(
        _fox_gate_body,
        grid=(bsz, s // tc),
        in_specs=[
            pl.BlockSpec((1, tc, LANES), lambda b, c: (b, c, 0)),
            pl.BlockSpec((1, LANES), lambda b, c: (0, 0)),
        ],
        out_specs=pl.BlockSpec((1, LANES, tc), lambda b, c: (b, 0, c)),
        out_shape=jax.ShapeDtypeStruct((bsz, LANES, s), F32),
        scratch_shapes=[pltpu.VMEM((LANES, 1), F32)],
        compiler_params=_params("parallel", "arbitrary"),
        name="fox_gates",
    )(gates, bias)


def _fox_attn_body(q_ref, k_ref, v_ref, cf_ref, y_ref, m_ref, l_ref, acc_ref):
    tq = q_ref.shape[1]
    qi = pl.program_id(2)
    scale = FOX_HEAD ** -0.5
    q = q_ref[0]
    m_ref[...] = jnp.full_like(m_ref, NEG_BIG)
    l_ref[...] = jnp.zeros_like(l_ref)
    acc_ref[...] = jnp.zeros_like(acc_ref)

    def block(j, masked):
        start = pl.multiple_of(j * tq, tq)
        kj = k_ref[0, pl.ds(start, tq), :]
        vj = v_ref[0, pl.ds(start, tq), :]
        s = _dot_nt(q, kj) * scale - cf_ref[0, 0, :, pl.ds(start, tq)]
        if masked:
            row = lax.broadcasted_iota(jnp.int32, (tq, tq), 0)
            col = lax.broadcasted_iota(jnp.int32, (tq, tq), 1)
            s = jnp.where(col <= row, s, NEG_BIG)
        m_old = m_ref[...]
        m_new = jnp.maximum(m_old, jnp.max(s, axis=1, keepdims=True))
        alpha = jnp.exp(m_old - m_new)
        p = jnp.exp(s - m_new)
        l_ref[...] = alpha * l_ref[...] + jnp.sum(p, axis=1, keepdims=True)
        acc_ref[...] = alpha * acc_ref[...] + _dot(p.astype(BF16), vj)
        m_ref[...] = m_new

    def loop_body(j, carry):
        block(j, masked=False)
        return carry

    lax.fori_loop(0, qi, loop_body, 0)
    block(qi, masked=True)
    y_ref[0] = (acc_ref[...] / l_ref[...]).astype(BF16)


def _fox_attn(p, cum, heads):
    bsz, s, _ = p.shape
    tq = min(FOX_TQ, s)
    assert s % tq == 0
    return pl.pallas_call(
        _fox_attn_body,
        grid=(bsz, heads, s // tq),
        in_specs=[
            pl.BlockSpec((1, tq, FOX_HEAD), lambda b, h, i: (b, i, h)),
            pl.BlockSpec((1, s, FOX_HEAD), lambda b, h, i: (b, 0, heads + h)),
            pl.BlockSpec((1, s, FOX_HEAD), lambda b, h, i: (b, 0, 2 * heads + h)),
            pl.BlockSpec((1, 1, 1, s), lambda b, h, i: (b, h, 0, 0)),
        ],
        out_specs=pl.BlockSpec((1, tq, FOX_HEAD), lambda b, h, i: (b, i, h)),
        out_shape=jax.ShapeDtypeStruct((bsz, s, heads * FOX_HEAD), BF16),
        scratch_shapes=[pltpu.VMEM((tq, 1), F32), pltpu.VMEM((tq, 1), F32), pltpu.VMEM((tq, FOX_HEAD), F32)],
        compiler_params=_params("parallel", "parallel", "arbitrary"),
        name="fox_attn",
    )(p, p, p, cum)


def _gla_intra_t(q, k, b):
    L = q.shape[0]
    lane_t = lax.broadcasted_iota(jnp.int32, (L, L), 1)
    a_t = jnp.zeros((L, L), F32)
    for t in range(L):
        rows = 8 * (t // 8 + 1)
        s_idx = lax.broadcasted_iota(jnp.int32, (rows, 1), 0)
        diff = jnp.where(s_idx <= t, b[t:t + 1, :] - b[:rows, :], NEG_BIG)
        w = jnp.exp(diff) * (k[:rows, :] * q[t:t + 1, :])
        colsum = jnp.sum(w, axis=1, keepdims=True)
        if rows < L:
            colsum = jnp.concatenate([colsum, jnp.zeros((L - rows, 1), F32)], axis=0)
        a_t = jnp.where(lane_t == t, colsum, a_t)
    return a_t


def _gla_body(q_ref, k_ref, v_ref, r_ref, alow_ref, wal_ref, bal_ref, hn_ref, y_ref, st_ref, *, heads):
    L = q_ref.shape[1]
    dk, dv = GLA_K, GLA_V
    qscale = dk ** -0.5

    @pl.when(pl.program_id(1) == 0)
    def _():
        st_ref[...] = jnp.zeros_like(st_ref)

    log_alpha = jax.nn.log_sigmoid(_dot_f32(alow_ref[0], wal_ref[...]) + bal_ref[...]) * (1.0 / GLA_GATE_TEMP)
    b_all = _cumsum_rows(log_alpha)

    for h in range(heads):
        b = b_all[:, h * dk:(h + 1) * dk]
        q = q_ref[0, :, h * dk:(h + 1) * dk].astype(F32) * qscale
        k = k_ref[0, :, h * dk:(h + 1) * dk].astype(F32)
        v = v_ref[0, :, h * dv:(h + 1) * dv]
        a_t = _gla_intra_t(q, k, b)
        st = st_ref[h]
        o = _dot_tn(a_t.astype(BF16), v) + _dot_nt((q * jnp.exp(b)).astype(BF16), st.astype(BF16))
        b_last = b[L - 1:L, :]
        k_dec = (k * jnp.exp(b_last - b)).astype(BF16)
        st_ref[h] = jnp.exp(b_last) * st + _dot_tn(v, k_dec)
        o = _rmsnorm(o, hn_ref[:, h * dv:(h + 1) * dv])
        r = r_ref[0, :, h * dv:(h + 1) * dv].astype(F32)
        y_ref[0, :, h * dv:(h + 1) * dv] = (o * (r * jax.nn.sigmoid(r))).astype(BF16)


def _gla_core(p, alow, w_alpha, b_alpha, hn, heads):
    bsz, s, _ = p.shape
    nk, nv = heads * GLA_K, heads * GLA_V
    L = min(GLA_CHUNK, s)
    assert s % L == 0 and nv == 2 * nk
    return pl.pallas_call(
        functools.partial(_gla_body, heads=heads),
        grid=(bsz, s // L),
        in_specs=[
            pl.BlockSpec((1, L, nk), lambda b, c: (b, c, 0)),
            pl.BlockSpec((1, L, nk), lambda b, c: (b, c, 1)),
            pl.BlockSpec((1, L, nv), lambda b, c: (b, c, 1)),
            pl.BlockSpec((1, L, nv), lambda b, c: (b, c, 2)),
            pl.BlockSpec((1, L, LANES), lambda b, c: (b, c, 0)),
            pl.BlockSpec((LANES, nk), lambda b, c: (0, 0)),
            pl.BlockSpec((1, nk), lambda b, c: (0, 0)),
            pl.BlockSpec((1, nv), lambda b, c: (0, 0)),
        ],
        out_specs=pl.BlockSpec((1, L, nv), lambda b, c: (b, c, 0)),
        out_shape=jax.ShapeDtypeStruct((bsz, s, nv), BF16),
        scratch_shapes=[pltpu.VMEM((heads, GLA_V, GLA_K), F32)],
        compiler_params=_params("parallel", "arbitrary"),
        name="gla_core",
    )(p, p, p, p, alow, w_alpha, b_alpha, hn)


def _pad_cols(w, width, offsets):
    out = jnp.zeros((w.shape[0], width), w.dtype)
    c0 = 0
    for off, n in offsets:
        out = out.at[:, off:off + n].set(w[:, c0:c0 + n])
        c0 += n
    return out


def _mlstm_layer(h, bsz, norm_g, w_in, b_if, head_norm, w_out):
    t, d = h.shape
    heads = d // MLSTM_V
    n_main = 2 * heads * MLSTM_QK + 2 * heads * MLSTM_V
    wg = _pad_cols(w_in[:, n_main:], 2 * LANES, [(0, heads), (LANES, heads)])
    bias = _pad_cols(b_if.reshape(1, -1), 2 * LANES, [(0, heads), (LANES, heads)])
    p, gates = _proj(h, norm_g, w_in[:, :n_main].astype(BF16), wg)
    y = _mlstm_core(p.reshape(bsz, t // bsz, n_main), gates.reshape(bsz, t // bsz, 2 * LANES), bias,
                    head_norm.reshape(1, -1), heads)
    return _out_proj(h, y.reshape(t, -1), w_out.astype(BF16))


def _fox_layer(h, bsz, norm_g, w_in, b_f, w_out):
    t, d = h.shape
    heads = d // FOX_HEAD
    n_main = 3 * heads * FOX_HEAD
    wg = _pad_cols(w_in[:, n_main:], LANES, [(0, heads)])
    bias = _pad_cols(b_f.reshape(1, -1), LANES, [(0, heads)])
    p, gates = _proj(h, norm_g, w_in[:, :n_main].astype(BF16), wg)
    s = t // bsz
    cum = _fox_gates(gates.reshape(bsz, s, LANES), bias)
    cum = cum[:, :heads, :].reshape(bsz, heads, 1, s)
    y = _fox_attn(p.reshape(bsz, s, n_main), cum, heads)
    return _out_proj(h, y.reshape(t, -1), w_out.astype(BF16))


def _gla_layer(h, bsz, norm_g, w_in, w_alpha, b_alpha, head_norm, w_out):
    t, d = h.shape
    heads = d // GLA_V
    n_main = 2 * heads * GLA_K + 2 * heads * GLA_V
    rank = w_alpha.shape[0]
    wg = _pad_cols(w_in[:, n_main:], LANES, [(0, rank)])
    wal = jnp.zeros((LANES, w_alpha.shape[1]), F32).at[:rank].set(w_alpha)
    p, alow = _proj(h, norm_g, w_in[:, :n_main].astype(BF16), wg)
    s = t // bsz
    y = _gla_core(p.reshape(bsz, s, n_main), alow.reshape(bsz, s, LANES), wal, b_alpha.reshape(1, -1),
                  head_norm.reshape(1, -1), heads)
    return _out_proj(h, y.reshape(t, -1), w_out.astype(BF16))


def kernel(x, ffn_norm, ffn_w_in, ffn_w_out, mix_norm, a_w_in, a_b_if, a_head_norm, a_w_out, b_w_in, b_b_f, b_w_out, c_w_in, c_w_alpha, c_b_alpha, c_head_norm, c_w_out, final_norm):
    bsz, s, d = x.shape
    depth = ffn_norm.shape[0]
    h = x.reshape(bsz * s, d)
    for i in range(depth):
        h = _ffn(h, ffn_norm[i, 0], ffn_w_in[i, 0].astype(BF16), ffn_w_out[i, 0].astype(BF16))
        kind, j = i % 3, i // 3
        if kind == 0:
            h = _mlstm_layer(h, bsz, mix_norm[i], a_w_in[j], a_b_if[j], a_head_norm[j], a_w_out[j])
        elif kind == 1:
            h = _fox_layer(h, bsz, mix_norm[i], b_w_in[j], b_b_f[j], b_w_out[j])
        else:
            h = _gla_layer(h, bsz, mix_norm[i], c_w_in[j], c_w_alpha[j], c_b_alpha[j], c_head_norm[j], c_w_out[j])
        h = _ffn(h, ffn_norm[i, 1], ffn_w_in[i, 1].astype(BF16), ffn_w_out[i, 1].astype(BF16),
                 final_g=final_norm if i == depth - 1 else None)
    return h.reshape(bsz, s, d)
```

```python
import functools

import jax
import jax.numpy as jnp
from jax import lax
from jax.experimental import pallas as pl
from jax.experimental.pallas import tpu as pltpu

F32 = jnp.float32
BF16 = jnp.bfloat16

RMS_EPS = 1e-6
FFN_RESIDUAL_WEIGHT = 0.5
GLA_GATE_TEMP = 16.0
NEG_BIG = -1e30

LANES = 128
MLSTM_QK, MLSTM_V = 128, 256
FOX_HEAD = 128
GLA_K, GLA_V = 256, 512

VMEM_LIMIT_BYTES = 60 * 1024 * 1024

FFN_TM, FFN_TF, FFN_ROWS = 1024, 512, 256
PROJ_TM, PROJ_TN = 1024, 1024
OUT_TM = 512
MLSTM_CHUNK = 256
FOX_TQ = 512
FOX_GATE_CHUNK = 512
GLA_CHUNK = 64


def _params(*sem):
    return pltpu.CompilerParams(dimension_semantics=sem, vmem_limit_bytes=VMEM_LIMIT_BYTES)


def _tile(n, pref):
    if n <= pref:
        return n
    return max(c for c in range(LANES, pref + 1, LANES) if n % c == 0)


def _dot(a, b):
    return jnp.dot(a, b, preferred_element_type=F32)


def _dot_nt(a, b):
    return lax.dot_general(a, b, (((1,), (1,)), ((), ())), preferred_element_type=F32)


def _dot_tn(a, b):
    return lax.dot_general(a, b, (((0,), (0,)), ((), ())), preferred_element_type=F32)


def _split2(x):
    hi = x.astype(BF16)
    lo = (x - hi.astype(F32)).astype(BF16)
    return hi, lo


def _split3(x):
    hi = x.astype(BF16)
    r = x - hi.astype(F32)
    mid = r.astype(BF16)
    lo = (r - mid.astype(F32)).astype(BF16)
    return hi, mid, lo


def _dot_f32(a, b):
    ah, al = _split2(a)
    bh, bl = _split2(b)
    return _dot(ah, bh) + (_dot(ah, bl) + _dot(al, bh))


def _cumsum_rows(x):
    n = x.shape[0]
    tril = (lax.broadcasted_iota(jnp.int32, (n, n), 0) >= lax.broadcasted_iota(jnp.int32, (n, n), 1)).astype(BF16)
    hi, mid, lo = _split3(x)
    return _dot(tril, hi) + (_dot(tril, mid) + _dot(tril, lo))


def _cumsum_lanes(x):
    n = x.shape[1]
    triu = (lax.broadcasted_iota(jnp.int32, (n, n), 0) <= lax.broadcasted_iota(jnp.int32, (n, n), 1)).astype(BF16)
    hi, mid, lo = _split3(x)
    return _dot(hi, triu) + (_dot(mid, triu) + _dot(lo, triu))


def _rmsnorm(x, g):
    return x * lax.rsqrt(jnp.mean(x * x, axis=-1, keepdims=True) + RMS_EPS) * g


def _ffn_body(x_ref, g_ref, wa_ref, wb_ref, wo_ref, fg_ref, o_ref, u_ref, *, final_norm):
    f = pl.program_id(1)
    tm = x_ref.shape[0]
    rs = min(FFN_ROWS, tm)
    row_tiles = [pl.ds(r * rs, rs) for r in range(tm // rs)]

    @pl.when(f == 0)
    def _():
        for rows in row_tiles:
            u_ref[rows, :] = _rmsnorm(x_ref[rows, :], g_ref[...]).astype(BF16)
            o_ref[rows, :] = jnp.zeros((rs, o_ref.shape[1]), F32)

    for rows in row_tiles:
        u = u_ref[rows, :]
        a = _dot(u, wa_ref[...])
        b = _dot(u, wb_ref[...])
        hid = (a * jax.nn.sigmoid(a) * b).astype(BF16)
        o_ref[rows, :] += _dot(hid, wo_ref[...])

    @pl.when(f == pl.num_programs(1) - 1)
    def _():
        for rows in row_tiles:
            r = x_ref[rows, :] + FFN_RESIDUAL_WEIGHT * o_ref[rows, :]
            if final_norm:
                r = _rmsnorm(r, fg_ref[...])
            o_ref[rows, :] = r


def _ffn(h, g, w_in, w_out, final_g=None):
    t, d = h.shape
    dff = w_out.shape[0]
    tm, tf = _tile(t, FFN_TM), _tile(dff, FFN_TF)
    assert t % tm == 0 and dff % tf == 0
    nf = dff // tf
    fg = g if final_g is None else final_g
    return pl.pallas_call(
        functools.partial(_ffn_body, final_norm=final_g is not None),
        grid=(t // tm, nf),
        in_specs=[
            pl.BlockSpec((tm, d), lambda i, f: (i, 0)),
            pl.BlockSpec((1, d), lambda i, f: (0, 0)),
            pl.BlockSpec((d, tf), lambda i, f: (0, f)),
            pl.BlockSpec((d, tf), lambda i, f: (0, f + nf)),
            pl.BlockSpec((tf, d), lambda i, f: (f, 0)),
            pl.BlockSpec((1, d), lambda i, f: (0, 0)),
        ],
        out_specs=pl.BlockSpec((tm, d), lambda i, f: (i, 0)),
        out_shape=jax.ShapeDtypeStruct((t, d), F32),
        scratch_shapes=[pltpu.VMEM((tm, d), BF16)],
        compiler_params=_params("parallel", "arbitrary"),
        name="ffn",
    )(h, g.reshape(1, d), w_in, w_in, w_out, fg.reshape(1, d))


def _proj_body(x_ref, g_ref, w_ref, wg_ref, p_ref, gate_ref, u_ref):
    @pl.when(pl.program_id(1) == 0)
    def _():
        tm = x_ref.shape[0]
        rs = min(FFN_ROWS, tm)
        for r in range(tm // rs):
            rows = pl.ds(r * rs, rs)
            u = _rmsnorm(x_ref[rows, :], g_ref[...])
            u_ref[rows, :] = u.astype(BF16)
            gate_ref[rows, :] = _dot_f32(u, wg_ref[...])

    p_ref[...] = _dot(u_ref[...], w_ref[...]).astype(BF16)


def _proj(h, g, w, wg):
    t, d = h.shape
    n, gw = w.shape[1], wg.shape[1]
    tm, tn = _tile(t, PROJ_TM), _tile(n, PROJ_TN)
    assert t % tm == 0 and n % tn == 0
    return pl.pallas_call(
        _proj_body,
        grid=(t // tm, n // tn),
        in_specs=[
            pl.BlockSpec((tm, d), lambda i, j: (i, 0)),
            pl.BlockSpec((1, d), lambda i, j: (0, 0)),
            pl.BlockSpec((d, tn), lambda i, j: (0, j)),
            pl.BlockSpec((d, gw), lambda i, j: (0, 0)),
        ],
        out_specs=[
            pl.BlockSpec((tm, tn), lambda i, j: (i, j)),
            pl.BlockSpec((tm, gw), lambda i, j: (i, 0)),
        ],
        out_shape=[jax.ShapeDtypeStruct((t, n), BF16), jax.ShapeDtypeStruct((t, gw), F32)],
        scratch_shapes=[pltpu.VMEM((tm, d), BF16)],
        compiler_params=_params("parallel", "arbitrary"),
        name="mixer_in_proj",
    )(h, g.reshape(1, d), w, wg)


def _out_body(h_ref, y_ref, w_ref, o_ref):
    o_ref[...] = h_ref[...] + _dot(y_ref[...], w_ref[...])


def _out_proj(h, y, w):
    t, d = h.shape
    k = y.shape[1]
    tm = min(OUT_TM, t)
    assert t % tm == 0
    return pl.pallas_call(
        _out_body,
        grid=(t // tm,),
        in_specs=[
            pl.BlockSpec((tm, d), lambda i: (i, 0)),
            pl.BlockSpec((tm, k), lambda i: (i, 0)),
            pl.BlockSpec((k, d), lambda i: (0, 0)),
        ],
        out_specs=pl.BlockSpec((tm, d), lambda i: (i, 0)),
        out_shape=jax.ShapeDtypeStruct((t, d), F32),
        compiler_params=_params("parallel"),
        name="mixer_out_proj",
    )(h, y, w)


def _mlstm_body(q_ref, k_ref, v_ref, o_ref, gate_ref, bias_ref, hn_ref, y_ref, c_ref, m_ref, *, heads):
    L = q_ref.shape[1]
    dqk, dv = MLSTM_QK, MLSTM_V
    scale = dqk ** -0.5

    @pl.when(pl.program_id(1) == 0)
    def _():
        c_ref[...] = jnp.zeros_like(c_ref)
        m_ref[...] = jnp.zeros_like(m_ref)

    gates = gate_ref[0] + bias_ref[...]
    i_pre = gates[:, :LANES]
    b_cum = _cumsum_rows(jax.nn.log_sigmoid(gates[:, LANES:]))
    b_t = b_cum.T
    i_t = i_pre.T
    m_all = m_ref[...]
    row = lax.broadcasted_iota(jnp.int32, (L, L), 0)
    col = lax.broadcasted_iota(jnp.int32, (L, L), 1)
    causal = col <= row
    ones_blk = jnp.ones((L, LANES), BF16)

    for h in range(heads):
        bcol, icol = b_cum[:, h:h + 1], i_pre[:, h:h + 1]
        brow, irow = b_t[h:h + 1, :], i_t[h:h + 1, :]
        m_prev = m_all[:, h:h + 1]
        d_log = jnp.where(causal, bcol - brow + irow, NEG_BIG)
        inter_log = bcol + m_prev
        m_t = jnp.maximum(inter_log, jnp.max(d_log, axis=1, keepdims=True))
        d_w = jnp.exp(d_log - m_t)
        inter_w = jnp.exp(inter_log - m_t)

        q = q_ref[0, :, h * dqk:(h + 1) * dqk]
        k = k_ref[0, :, h * dqk:(h + 1) * dqk]
        v_ext = jnp.concatenate([v_ref[0, :, h * dv:(h + 1) * dv], ones_blk], axis=1)
        s = _dot_nt(q, k) * (d_w * scale)
        c_prev = c_ref[h]
        nd = _dot(s.astype(BF16), v_ext) + inter_w * _dot(q, c_prev.astype(BF16))
        num, den = nd[:, :dv], nd[:, dv:dv + 1]
        hh = num / jnp.maximum(jnp.abs(den), jnp.exp(-m_t))
        hh = _rmsnorm(hh, hn_ref[:, h * dv:(h + 1) * dv])
        gate_o = jax.nn.sigmoid(o_ref[0, :, h * dv:(h + 1) * dv].astype(F32))
        y_ref[0, :, h * dv:(h + 1) * dv] = (hh * gate_o).astype(BF16)

        b_last = bcol[L - 1:L, :]
        g = b_last - bcol + icol
        m_new = jnp.maximum(b_last + m_prev, jnp.max(g, axis=0, keepdims=True))
        w_c = jnp.exp(b_last + m_prev - m_new)
        w_s = jnp.exp(g - m_new)
        k_w = (k.astype(F32) * (w_s * scale)).astype(BF16)
        c_ref[h] = w_c * c_prev + _dot_tn(k_w, v_ext)
        m_ref[:, h:h + 1] = m_new


def _mlstm_core(p, gates, bias, hn, heads):
    bsz, s, _ = p.shape
    nq, nv = heads * MLSTM_QK, heads * MLSTM_V
    L = min(MLSTM_CHUNK, s)
    assert s % L == 0 and nv == 2 * nq
    return pl.pallas_call(
        functools.partial(_mlstm_body, heads=heads),
        grid=(bsz, s // L),
        in_specs=[
            pl.BlockSpec((1, L, nq), lambda b, c: (b, c, 0)),
            pl.BlockSpec((1, L, nq), lambda b, c: (b, c, 1)),
            pl.BlockSpec((1, L, nv), lambda b, c: (b, c, 1)),
            pl.BlockSpec((1, L, nv), lambda b, c: (b, c, 2)),
            pl.BlockSpec((1, L, 2 * LANES), lambda b, c: (b, c, 0)),
            pl.BlockSpec((1, 2 * LANES), lambda b, c: (0, 0)),
            pl.BlockSpec((1, nv), lambda b, c: (0, 0)),
        ],
        out_specs=pl.BlockSpec((1, L, nv), lambda b, c: (b, c, 0)),
        out_shape=jax.ShapeDtypeStruct((bsz, s, nv), BF16),
        scratch_shapes=[pltpu.VMEM((heads, MLSTM_QK, MLSTM_V + LANES), F32), pltpu.VMEM((1, LANES), F32)],
        compiler_params=_params("parallel", "arbitrary"),
        name="mlstm_core",
    )(p, p, p, p, gates, bias, hn)


FOX_BIAS_PIECES = 3


def _fox_gate_body(gate_ref, bias_ref, cp_ref, carry_ref, *, heads):
    @pl.when(pl.program_id(1) == 0)
    def _():
        carry_ref[...] = jnp.zeros_like(carry_ref)

    log_f = jax.nn.log_sigmoid(gate_ref[0] + bias_ref[...])
    cum = _cumsum_rows(log_f) + carry_ref[...]
    tc = cum.shape[0]
    carry_ref[...] = cum[tc - 1:tc, :]
    lane = lax.broadcasted_iota(jnp.int32, (tc, LANES), 1)
    inv_scale = -(FOX_HEAD ** 0.5)
    for h in range(heads):
        c = jnp.broadcast_to(cum[:, h:h + 1] * inv_scale, (tc, LANES))
        hi, mid, lo = (x.astype(F32) for x in _split3(c))
        pieces = jnp.where(lane == 0, hi, jnp.where(lane == 1, mid, jnp.where(lane == 2, lo, 0.0)))
        cp_ref[0, h] = pieces.astype(BF16)


def _fox_gates(gates, bias, heads):
    bsz, s, _ = gates.shape
    tc = min(FOX_GATE_CHUNK, s)
    assert s % tc == 0
    return pl.pallas_call(
        functools.partial(_fox_gate_body, heads=heads),
        grid=(bsz, s // tc),
        in_specs=[
            pl.BlockSpec((1, tc, LANES), lambda b, c: (b, c, 0)),
            pl.BlockSpec((1, LANES), lambda b, c: (0, 0)),
        ],
        out_specs=pl.BlockSpec((1, heads, tc, LANES), lambda b, c: (b, 0, c, 0)),
        out_shape=jax.ShapeDtypeStruct((bsz, heads, s, LANES), BF16),
        scratch_shapes=[pltpu.VMEM((1, LANES), F32)],
        compiler_params=_params("parallel", "arbitrary"),
        name="fox_gates",
    )(gates, bias)


def _fox_attn_body(q_ref, k_ref, v_ref, cp_ref, y_ref, m_ref, l_ref, acc_ref):
    tq = q_ref.shape[1]
    qi = pl.program_id(2)
    exp2_scale = (FOX_HEAD ** -0.5) * 1.4426950408889634
    lane = lax.broadcasted_iota(jnp.int32, (tq, LANES), 1)
    ones_cols = jnp.where(lane < FOX_BIAS_PIECES, 1.0, 0.0).astype(BF16)
    q_ext = jnp.concatenate([q_ref[0], ones_cols], axis=1)
    m_ref[...] = jnp.full_like(m_ref, NEG_BIG)
    l_ref[...] = jnp.zeros_like(l_ref)
    acc_ref[...] = jnp.zeros_like(acc_ref)

    def block(start, tk, masked):
        rows = pl.ds(start, tk)
        k_ext = jnp.concatenate([k_ref[0, rows, :], cp_ref[0, 0, rows, :]], axis=1)
        s = _dot_nt(k_ext, q_ext)
        if masked:
            key = lax.broadcasted_iota(jnp.int32, (tk, tq), 0)
            qry = lax.broadcasted_iota(jnp.int32, (tk, tq), 1)
            s = jnp.where(key <= qry, s, NEG_BIG)
        m_old = m_ref[...]
        m_new = jnp.maximum(m_old, jnp.max(s, axis=0, keepdims=True))
        alpha = jnp.exp2((m_old - m_new) * exp2_scale)
        p = jnp.exp2((s - m_new) * exp2_scale)
        l_ref[...] = alpha * l_ref[...] + jnp.sum(p, axis=0, keepdims=True)
        acc_ref[...] = alpha * acc_ref[...] + _dot_tn(v_ref[0, rows, :], p.astype(BF16))
        m_ref[...] = m_new

    def pair_body(j, carry):
        block(pl.multiple_of(j * (2 * tq), 2 * tq), 2 * tq, masked=False)
        return carry

    lax.fori_loop(0, lax.shift_right_logical(qi, jnp.int32(1)), pair_body, 0)

    @pl.when(lax.bitwise_and(qi, jnp.int32(1)) == 1)
    def _():
        block(pl.multiple_of((qi - 1) * tq, tq), tq, masked=False)

    block(pl.multiple_of(qi * tq, tq), tq, masked=True)
    y_ref[0] = (acc_ref[...] / l_ref[...]).T.astype(BF16)


def _fox_attn(p, cp, heads):
    bsz, s, _ = p.shape
    tq = min(FOX_TQ, s)
    assert s % tq == 0
    return pl.pallas_call(
        _fox_attn_body,
        grid=(bsz, heads, s // tq),
        in_specs=[
            pl.BlockSpec((1, tq, FOX_HEAD), lambda b, h, i: (b, i, h)),
            pl.BlockSpec((1, s, FOX_HEAD), lambda b, h, i: (b, 0, heads + h)),
            pl.BlockSpec((1, s, FOX_HEAD), lambda b, h, i: (b, 0, 2 * heads + h)),
            pl.BlockSpec((1, 1, s, LANES), lambda b, h, i: (b, h, 0, 0)),
        ],
        out_specs=pl.BlockSpec((1, tq, FOX_HEAD), lambda b, h, i: (b, i, h)),
        out_shape=jax.ShapeDtypeStruct((bsz, s, heads * FOX_HEAD), BF16),
        scratch_shapes=[pltpu.VMEM((1, tq), F32), pltpu.VMEM((1, tq), F32), pltpu.VMEM((FOX_HEAD, tq), F32)],
        compiler_params=_params("parallel", "parallel", "arbitrary"),
        name="fox_attn",
    )(p, p, p, cp)


def _gla_intra_t(q, k, b):
    L = q.shape[0]
    lane_t = lax.broadcasted_iota(jnp.int32, (L, L), 1)
    a_t = jnp.zeros((L, L), F32)
    for t in range(L):
        rows = 8 * (t // 8 + 1)
        s_idx = lax.broadcasted_iota(jnp.int32, (rows, 1), 0)
        diff = jnp.where(s_idx <= t, b[t:t + 1, :] - b[:rows, :], NEG_BIG)
        w = jnp.exp(diff) * (k[:rows, :] * q[t:t + 1, :])
        colsum = jnp.sum(w, axis=1, keepdims=True)
        if rows < L:
            colsum = jnp.concatenate([colsum, jnp.zeros((L - rows, 1), F32)], axis=0)
        a_t = jnp.where(lane_t == t, colsum, a_t)
    return a_t


def _gla_body(q_ref, k_ref, v_ref, r_ref, alow_ref, wal_ref, bal_ref, hn_ref, y_ref, st_ref, *, heads):
    L = q_ref.shape[1]
    dk, dv = GLA_K, GLA_V
    qscale = dk ** -0.5

    @pl.when(pl.program_id(1) == 0)
    def _():
        st_ref[...] = jnp.zeros_like(st_ref)

    log_alpha = jax.nn.log_sigmoid(_dot_f32(alow_ref[0], wal_ref[...]) + bal_ref[...]) * (1.0 / GLA_GATE_TEMP)
    b_all = _cumsum_rows(log_alpha)

    for h in range(heads):
        b = b_all[:, h * dk:(h + 1) * dk]
        q = q_ref[0, :, h * dk:(h + 1) * dk].astype(F32) * qscale
        k = k_ref[0, :, h * dk:(h + 1) * dk].astype(F32)
        v = v_ref[0, :, h * dv:(h + 1) * dv]
        a_t = _gla_intra_t(q, k, b)
        st = st_ref[h]
        o = _dot_tn(a_t.astype(BF16), v) + _dot_nt((q * jnp.exp(b)).astype(BF16), st.astype(BF16))
        b_last = b[L - 1:L, :]
        k_dec = (k * jnp.exp(b_last - b)).astype(BF16)
        st_ref[h] = jnp.exp(b_last) * st + _dot_tn(v, k_dec)
        o = _rmsnorm(o, hn_ref[:, h * dv:(h + 1) * dv])
        r = r_ref[0, :, h * dv:(h + 1) * dv].astype(F32)
        y_ref[0, :, h * dv:(h + 1) * dv] = (o * (r * jax.nn.sigmoid(r))).astype(BF16)


def _gla_core(p, alow, w_alpha, b_alpha, hn, heads):
    bsz, s, _ = p.shape
    nk, nv = heads * GLA_K, heads * GLA_V
    L = min(GLA_CHUNK, s)
    assert s % L == 0 and nv == 2 * nk
    return pl.pallas_call(
        functools.partial(_gla_body, heads=heads),
        grid=(bsz, s // L),
        in_specs=[
            pl.BlockSpec((1, L, nk), lambda b, c: (b, c, 0)),
            pl.BlockSpec((1, L, nk), lambda b, c: (b, c, 1)),
            pl.BlockSpec((1, L, nv), lambda b, c: (b, c, 1)),
            pl.BlockSpec((1, L, nv), lambda b, c: (b, c, 2)),
            pl.BlockSpec((1, L, LANES), lambda b, c: (b, c, 0)),
            pl.BlockSpec((LANES, nk), lambda b, c: (0, 0)),
            pl.BlockSpec((1, nk), lambda b, c: (0, 0)),
            pl.BlockSpec((1, nv), lambda b, c: (0, 0)),
        ],
        out_specs=pl.BlockSpec((1, L, nv), lambda b, c: (b, c, 0)),
        out_shape=jax.ShapeDtypeStruct((bsz, s, nv), BF16),
        scratch_shapes=[pltpu.VMEM((heads, GLA_V, GLA_K), F32)],
        compiler_params=_params("parallel", "arbitrary"),
        name="gla_core",
    )(p, p, p, p, alow, w_alpha, b_alpha, hn)


def _pad_cols(w, width, offsets):
    out = jnp.zeros((w.shape[0], width), w.dtype)
    c0 = 0
    for off, n in offsets:
        out = out.at[:, off:off + n].set(w[:, c0:c0 + n])
        c0 += n
    return out


def _mlstm_layer(h, bsz, norm_g, w_in, b_if, head_norm, w_out):
    t, d = h.shape
    heads = d // MLSTM_V
    n_main = 2 * heads * MLSTM_QK + 2 * heads * MLSTM_V
    wg = _pad_cols(w_in[:, n_main:], 2 * LANES, [(0, heads), (LANES, heads)])
    bias = _pad_cols(b_if.reshape(1, -1), 2 * LANES, [(0, heads), (LANES, heads)])
    p, gates = _proj(h, norm_g, w_in[:, :n_main].astype(BF16), wg)
    y = _mlstm_core(p.reshape(bsz, t // bsz, n_main), gates.reshape(bsz, t // bsz, 2 * LANES), bias,
                    head_norm.reshape(1, -1), heads)
    return _out_proj(h, y.reshape(t, -1), w_out.astype(BF16))


def _fox_layer(h, bsz, norm_g, w_in, b_f, w_out):
    t, d = h.shape
    heads = d // FOX_HEAD
    n_main = 3 * heads * FOX_HEAD
    wg = _pad_cols(w_in[:, n_main:], LANES, [(0, heads)])
    bias = _pad_cols(b_f.reshape(1, -1), LANES, [(0, heads)])
    p, gates = _proj(h, norm_g, w_in[:, :n_main].astype(BF16), wg)
    s = t // bsz
    cp = _fox_gates(gates.reshape(bsz, s, LANES), bias, heads)
    y = _fox_attn(p.reshape(bsz, s, n_main), cp, heads)
    return _out_proj(h, y.reshape(t, -1), w_out.astype(BF16))


def _gla_layer(h, bsz, norm_g, w_in, w_alpha, b_alpha, head_norm, w_out):
    t, d = h.shape
    heads = d // GLA_V
    n_main = 2 * heads * GLA_K + 2 * heads * GLA_V
    rank = w_alpha.shape[0]
    wg = _pad_cols(w_in[:, n_main:], LANES, [(0, rank)])
    wal = jnp.zeros((LANES, w_alpha.shape[1]), F32).at[:rank].set(w_alpha)
    p, alow = _proj(h, norm_g, w_in[:, :n_main].astype(BF16), wg)
    s = t // bsz
    y = _gla_core(p.reshape(bsz, s, n_main), alow.reshape(bsz, s, LANES), wal, b_alpha.reshape(1, -1),
                  head_norm.reshape(1, -1), heads)
    return _out_proj(h, y.reshape(t, -1), w_out.astype(BF16))


def kernel(x, ffn_norm, ffn_w_in, ffn_w_out, mix_norm, a_w_in, a_b_if, a_head_norm, a_w_out, b_w_in, b_b_f, b_w_out, c_w_in, c_w_alpha, c_b_alpha, c_head_norm, c_w_out, final_norm):
    bsz, s, d = x.shape
    depth = ffn_norm.shape[0]
    h = x.reshape(bsz * s, d)
    for i in range(depth):
        h = _ffn(h, ffn_norm[i, 0], ffn_w_in[i, 0].astype(BF16), ffn_w_out[i, 0].astype(BF16))
        kind, j = i % 3, i // 3
        if kind == 0:
            h = _mlstm_layer(h, bsz, mix_norm[i], a_w_in[j], a_b_if[j], a_head_norm[j], a_w_out[j])
        elif kind == 1:
            h = _fox_layer(h, bsz, mix_norm[i], b_w_in[j], b_b_f[j], b_w_out[j])
        else:
            h = _gla_layer(h, bsz, mix_norm[i], c_w_in[j], c_w_alpha[j], c_b_alpha[j], c_head_norm[j], c_w_out[j])
        h = _ffn(h, ffn_norm[i, 1], ffn_w_in[i, 1].astype(BF16), ffn_w_out[i, 1].astype(BF16),
                 final_g=final_norm if i == depth - 1 else None)
    return h.reshape(bsz, s, d)
```

```python
import functools

import jax
import jax.numpy as jnp
from jax import lax
from jax.experimental import pallas as pl
from jax.experimental.pallas import tpu as pltpu

F32 = jnp.float32
BF16 = jnp.bfloat16

RMS_EPS = 1e-6
FFN_RESIDUAL_WEIGHT = 0.5
GLA_GATE_TEMP = 16.0
NEG_BIG = -1e30

LANES = 128
MLSTM_QK, MLSTM_V = 128, 256
FOX_HEAD = 128
GLA_K, GLA_V = 256, 512

VMEM_LIMIT_BYTES = 60 * 1024 * 1024

FFN_TM, FFN_TF, FFN_ROWS = 1024, 512, 512
PROJ_TM, PROJ_TN, PROJ_ROWS = 1024, 1536, 512
OUT_TM = 512
MLSTM_CHUNK = 256
FOX_TQ = 512
FOX_HEADS_PER_STEP = 2
FOX_GATE_CHUNK = 512
GLA_CHUNK = 64
GLA_SUB = 16


def _params(*sem):
    return pltpu.CompilerParams(dimension_semantics=sem, vmem_limit_bytes=VMEM_LIMIT_BYTES)


def _tile(n, pref):
    if n <= pref:
        return n
    return max(c for c in range(LANES, pref + 1, LANES) if n % c == 0)


def _dot(a, b):
    return jnp.dot(a, b, preferred_element_type=F32)


def _dot_nt(a, b):
    return lax.dot_general(a, b, (((1,), (1,)), ((), ())), preferred_element_type=F32)


def _dot_tn(a, b):
    return lax.dot_general(a, b, (((0,), (0,)), ((), ())), preferred_element_type=F32)


def _split2(x):
    hi = x.astype(BF16)
    lo = (x - hi.astype(F32)).astype(BF16)
    return hi, lo


def _split3(x):
    hi = x.astype(BF16)
    r = x - hi.astype(F32)
    mid = r.astype(BF16)
    lo = (r - mid.astype(F32)).astype(BF16)
    return hi, mid, lo


def _dot_f32(a, b):
    ah, al = _split2(a)
    bh, bl = _split2(b)
    return _dot(ah, bh) + (_dot(ah, bl) + _dot(al, bh))


def _cumsum_rows(x):
    n = x.shape[0]
    tril = (lax.broadcasted_iota(jnp.int32, (n, n), 0) >= lax.broadcasted_iota(jnp.int32, (n, n), 1)).astype(BF16)
    hi, mid, lo = _split3(x)
    return _dot(tril, hi) + (_dot(tril, mid) + _dot(tril, lo))


def _cumsum_lanes(x):
    n = x.shape[1]
    triu = (lax.broadcasted_iota(jnp.int32, (n, n), 0) <= lax.broadcasted_iota(jnp.int32, (n, n), 1)).astype(BF16)
    hi, mid, lo = _split3(x)
    return _dot(hi, triu) + (_dot(mid, triu) + _dot(lo, triu))


def _rmsnorm(x, g):
    return x * lax.rsqrt(jnp.mean(x * x, axis=-1, keepdims=True) + RMS_EPS) * g


def _ffn_body(x_ref, g_ref, wa_ref, wb_ref, wo_ref, fg_ref, o_ref, u_ref, *, nf, final_norm):
    f = pl.program_id(1)
    tm = x_ref.shape[0]
    rs = min(FFN_ROWS, tm)
    row_tiles = [pl.ds(r * rs, rs) for r in range(tm // rs)]

    def step(first, last):
        for rows in row_tiles:
            if first:
                u = _rmsnorm(x_ref[rows, :], g_ref[...]).astype(BF16)
                u_ref[rows, :] = u
            else:
                u = u_ref[rows, :]
            a = _dot(u, wa_ref[...])
            b = _dot(u, wb_ref[...])
            hid = (a * jax.nn.sigmoid(a) * b).astype(BF16)
            acc = _dot(hid, wo_ref[...])
            if not first:
                acc = o_ref[rows, :] + acc
            if last:
                acc = x_ref[rows, :] + FFN_RESIDUAL_WEIGHT * acc
                if final_norm:
                    acc = _rmsnorm(acc, fg_ref[...])
            o_ref[rows, :] = acc

    if nf == 1:
        step(True, True)
    else:
        pl.when(f == 0)(lambda: step(True, False))
        if nf > 2:
            pl.when(jnp.logical_and(f > 0, f < nf - 1))(lambda: step(False, False))
        pl.when(f == nf - 1)(lambda: step(False, True))


def _ffn(h, g, w_in, w_out, final_g=None):
    t, d = h.shape
    dff = w_out.shape[0]
    tm, tf = _tile(t, FFN_TM), _tile(dff, FFN_TF)
    assert t % tm == 0 and dff % tf == 0
    nf = dff // tf
    fg = g if final_g is None else final_g
    return pl.pallas_call(
        functools.partial(_ffn_body, nf=nf, final_norm=final_g is not None),
        grid=(t // tm, nf),
        in_specs=[
            pl.BlockSpec((tm, d), lambda i, f: (i, 0)),
            pl.BlockSpec((1, d), lambda i, f: (0, 0)),
            pl.BlockSpec((d, tf), lambda i, f: (0, f)),
            pl.BlockSpec((d, tf), lambda i, f: (0, f + nf)),
            pl.BlockSpec((tf, d), lambda i, f: (f, 0)),
            pl.BlockSpec((1, d), lambda i, f: (0, 0)),
        ],
        out_specs=pl.BlockSpec((tm, d), lambda i, f: (i, 0)),
        out_shape=jax.ShapeDtypeStruct((t, d), F32),
        scratch_shapes=[pltpu.VMEM((tm, d), BF16)],
        compiler_params=_params("parallel", "arbitrary"),
        name="ffn",
    )(h, g.reshape(1, d), w_in, w_in, w_out, fg.reshape(1, d))


def _proj_body(x_ref, g_ref, w_ref, wg_ref, p_ref, gate_ref, u_ref):
    tm = x_ref.shape[0]
    rs = min(PROJ_ROWS, tm)

    def step(first):
        for r in range(tm // rs):
            rows = pl.ds(r * rs, rs)
            if first:
                u = _rmsnorm(x_ref[rows, :], g_ref[...]).astype(BF16)
                u_ref[rows, :] = u
                gate_ref[rows, :] = _dot(u, wg_ref[...])
            else:
                u = u_ref[rows, :]
            p_ref[rows, :] = _dot(u, w_ref[...]).astype(BF16)

    pl.when(pl.program_id(1) == 0)(lambda: step(True))
    pl.when(pl.program_id(1) > 0)(lambda: step(False))


def _proj(h, g, w, wg):
    t, d = h.shape
    n, gw = w.shape[1], wg.shape[1]
    tm, tn = _tile(t, PROJ_TM), _tile(n, PROJ_TN)
    assert t % tm == 0 and n % tn == 0
    return pl.pallas_call(
        _proj_body,
        grid=(t // tm, n // tn),
        in_specs=[
            pl.BlockSpec((tm, d), lambda i, j: (i, 0)),
            pl.BlockSpec((1, d), lambda i, j: (0, 0)),
            pl.BlockSpec((d, tn), lambda i, j: (0, j)),
            pl.BlockSpec((d, gw), lambda i, j: (0, 0)),
        ],
        out_specs=[
            pl.BlockSpec((tm, tn), lambda i, j: (i, j)),
            pl.BlockSpec((tm, gw), lambda i, j: (i, 0)),
        ],
        out_shape=[jax.ShapeDtypeStruct((t, n), BF16), jax.ShapeDtypeStruct((t, gw), F32)],
        scratch_shapes=[pltpu.VMEM((tm, d), BF16)],
        compiler_params=_params("parallel", "arbitrary"),
        name="mixer_in_proj",
    )(h, g.reshape(1, d), w, wg)


def _out_body(h_ref, y_ref, w_ref, o_ref):
    o_ref[...] = h_ref[...] + _dot(y_ref[...], w_ref[...])


def _out_proj(h, y, w):
    t, d = h.shape
    k = y.shape[1]
    tm = min(OUT_TM, t)
    assert t % tm == 0
    return pl.pallas_call(
        _out_body,
        grid=(t // tm,),
        in_specs=[
            pl.BlockSpec((tm, d), lambda i: (i, 0)),
            pl.BlockSpec((tm, k), lambda i: (i, 0)),
            pl.BlockSpec((k, d), lambda i: (0, 0)),
        ],
        out_specs=pl.BlockSpec((tm, d), lambda i: (i, 0)),
        out_shape=jax.ShapeDtypeStruct((t, d), F32),
        compiler_params=_params("parallel"),
        name="mixer_out_proj",
    )(h, y, w)


def _mlstm_body(q_ref, k_ref, v_ref, o_ref, gate_ref, bias_ref, hn_ref, y_ref, c_ref, m_ref, *, heads):
    L = q_ref.shape[1]
    dqk, dv = MLSTM_QK, MLSTM_V
    scale = dqk ** -0.5

    @pl.when(pl.program_id(1) == 0)
    def _():
        c_ref[...] = jnp.zeros_like(c_ref)
        m_ref[...] = jnp.zeros_like(m_ref)

    gates = gate_ref[0] + bias_ref[...]
    i_pre = gates[:, :LANES]
    b_cum = _cumsum_rows(jax.nn.log_sigmoid(gates[:, LANES:]))
    b_t = b_cum.T
    i_t = i_pre.T
    m_all = m_ref[...]
    row = lax.broadcasted_iota(jnp.int32, (L, L), 0)
    col = lax.broadcasted_iota(jnp.int32, (L, L), 1)
    causal = col <= row
    ones_blk = jnp.ones((L, LANES), BF16)

    for h in range(heads):
        bcol, icol = b_cum[:, h:h + 1], i_pre[:, h:h + 1]
        brow, irow = b_t[h:h + 1, :], i_t[h:h + 1, :]
        m_prev = m_all[:, h:h + 1]
        d_log = jnp.where(causal, bcol - brow + irow, NEG_BIG)
        inter_log = bcol + m_prev
        m_t = jnp.maximum(inter_log, jnp.max(d_log, axis=1, keepdims=True))
        d_w = jnp.exp(d_log - m_t)
        inter_w = jnp.exp(inter_log - m_t)

        q = q_ref[0, :, h * dqk:(h + 1) * dqk]
        k = k_ref[0, :, h * dqk:(h + 1) * dqk]
        v_ext = jnp.concatenate([v_ref[0, :, h * dv:(h + 1) * dv], ones_blk], axis=1)
        s = _dot_nt(q, k) * (d_w * scale)
        c_prev = c_ref[h]
        nd = _dot(s.astype(BF16), v_ext) + inter_w * _dot(q, c_prev.astype(BF16))
        num, den = nd[:, :dv], nd[:, dv:dv + 1]
        hh = num / jnp.maximum(jnp.abs(den), jnp.exp(-m_t))
        hh = _rmsnorm(hh, hn_ref[:, h * dv:(h + 1) * dv])
        gate_o = jax.nn.sigmoid(o_ref[0, :, h * dv:(h + 1) * dv].astype(F32))
        y_ref[0, :, h * dv:(h + 1) * dv] = (hh * gate_o).astype(BF16)

        b_last = bcol[L - 1:L, :]
        g = b_last - bcol + icol
        m_new = jnp.maximum(b_last + m_prev, jnp.max(g, axis=0, keepdims=True))
        w_c = jnp.exp(b_last + m_prev - m_new)
        w_s = jnp.exp(g - m_new)
        k_w = (k.astype(F32) * (w_s * scale)).astype(BF16)
        c_ref[h] = w_c * c_prev + _dot_tn(k_w, v_ext)
        m_ref[:, h:h + 1] = m_new


def _mlstm_core(p, gates, bias, hn, heads):
    bsz, s, _ = p.shape
    nq, nv = heads * MLSTM_QK, heads * MLSTM_V
    L = min(MLSTM_CHUNK, s)
    assert s % L == 0 and nv == 2 * nq
    return pl.pallas_call(
        functools.partial(_mlstm_body, heads=heads),
        grid=(bsz, s // L),
        in_specs=[
            pl.BlockSpec((1, L, nq), lambda b, c: (b, c, 0)),
            pl.BlockSpec((1, L, nq), lambda b, c: (b, c, 1)),
            pl.BlockSpec((1, L, nv), lambda b, c: (b, c, 1)),
            pl.BlockSpec((1, L, nv), lambda b, c: (b, c, 2)),
            pl.BlockSpec((1, L, 2 * LANES), lambda b, c: (b, c, 0)),
            pl.BlockSpec((1, 2 * LANES), lambda b, c: (0, 0)),
            pl.BlockSpec((1, nv), lambda b, c: (0, 0)),
        ],
        out_specs=pl.BlockSpec((1, L, nv), lambda b, c: (b, c, 0)),
        out_shape=jax.ShapeDtypeStruct((bsz, s, nv), BF16),
        scratch_shapes=[pltpu.VMEM((heads, MLSTM_QK, MLSTM_V + LANES), F32), pltpu.VMEM((1, LANES), F32)],
        compiler_params=_params("parallel", "arbitrary"),
        name="mlstm_core",
    )(p, p, p, p, gates, bias, hn)


FOX_BIAS_PIECES = 3


def _fox_gate_body(gate_ref, bias_ref, cp_ref, carry_ref, *, heads):
    @pl.when(pl.program_id(1) == 0)
    def _():
        carry_ref[...] = jnp.zeros_like(carry_ref)

    log_f = jax.nn.log_sigmoid(gate_ref[0] + bias_ref[...])
    cum = _cumsum_rows(log_f) + carry_ref[...]
    tc = cum.shape[0]
    carry_ref[...] = cum[tc - 1:tc, :]
    lane = lax.broadcasted_iota(jnp.int32, (tc, LANES), 1)
    inv_scale = -(FOX_HEAD ** 0.5)
    for h in range(heads):
        c = jnp.broadcast_to(cum[:, h:h + 1] * inv_scale, (tc, LANES))
        hi, mid, lo = (x.astype(F32) for x in _split3(c))
        pieces = jnp.where(lane == 0, hi, jnp.where(lane == 1, mid, jnp.where(lane == 2, lo, 0.0)))
        cp_ref[0, h] = pieces.astype(BF16)


def _fox_gates(gates, bias, heads):
    bsz, s, _ = gates.shape
    tc = min(FOX_GATE_CHUNK, s)
    assert s % tc == 0
    return pl.pallas_call(
        functools.partial(_fox_gate_body, heads=heads),
        grid=(bsz, s // tc),
        in_specs=[
            pl.BlockSpec((1, tc, LANES), lambda b, c: (b, c, 0)),
            pl.BlockSpec((1, LANES), lambda b, c: (0, 0)),
        ],
        out_specs=pl.BlockSpec((1, heads, tc, LANES), lambda b, c: (b, 0, c, 0)),
        out_shape=jax.ShapeDtypeStruct((bsz, heads, s, LANES), BF16),
        scratch_shapes=[pltpu.VMEM((1, LANES), F32)],
        compiler_params=_params("parallel", "arbitrary"),
        name="fox_gates",
    )(gates, bias)


def _fox_attn_body(q_ref, k_ref, v_ref, cp_ref, y_ref, m_ref, l_ref, acc_ref):
    tq = q_ref.shape[1]
    hps = q_ref.shape[2] // FOX_HEAD
    qi = pl.program_id(2)
    exp2_scale = (FOX_HEAD ** -0.5) * 1.4426950408889634
    lane = lax.broadcasted_iota(jnp.int32, (tq, LANES), 1)
    ones_cols = jnp.where(lane < FOX_BIAS_PIECES, 1.0, 0.0).astype(BF16)
    heads = [pl.ds(g * FOX_HEAD, FOX_HEAD) for g in range(hps)]
    q_ext = [jnp.concatenate([q_ref[0, :, cols], ones_cols], axis=1) for cols in heads]
    m_ref[...] = jnp.full_like(m_ref, NEG_BIG)
    l_ref[...] = jnp.zeros_like(l_ref)
    acc_ref[...] = jnp.zeros_like(acc_ref)

    def block(start, tk, masked):
        rows = pl.ds(start, tk)
        for g, cols in enumerate(heads):
            k_ext = jnp.concatenate([k_ref[0, rows, cols], cp_ref[0, g, rows, :]], axis=1)
            s = _dot_nt(k_ext, q_ext[g])
            if masked:
                key = lax.broadcasted_iota(jnp.int32, (tk, tq), 0)
                qry = lax.broadcasted_iota(jnp.int32, (tk, tq), 1)
                s = jnp.where(key <= qry, s, NEG_BIG)
            m_old = m_ref[g]
            m_new = jnp.maximum(m_old, jnp.max(s, axis=0, keepdims=True))
            alpha = jnp.exp2((m_old - m_new) * exp2_scale)
            p = jnp.exp2((s - m_new) * exp2_scale)
            l_ref[g] = alpha * l_ref[g] + jnp.sum(p, axis=0, keepdims=True)
            acc_ref[g] = alpha * acc_ref[g] + _dot_tn(v_ref[0, rows, cols], p.astype(BF16))
            m_ref[g] = m_new

    def pair_body(j, carry):
        block(pl.multiple_of(j * (2 * tq), 2 * tq), 2 * tq, masked=False)
        return carry

    lax.fori_loop(0, lax.shift_right_logical(qi, jnp.int32(1)), pair_body, 0)

    @pl.when(lax.bitwise_and(qi, jnp.int32(1)) == 1)
    def _():
        block(pl.multiple_of((qi - 1) * tq, tq), tq, masked=False)

    block(pl.multiple_of(qi * tq, tq), tq, masked=True)
    for g, cols in enumerate(heads):
        y_ref[0, :, cols] = (acc_ref[g] / l_ref[g]).T.astype(BF16)


def _fox_attn(p, cp, heads):
    bsz, s, _ = p.shape
    tq = min(FOX_TQ, s)
    hps = FOX_HEADS_PER_STEP
    assert s % tq == 0 and heads % hps == 0
    hg, w = heads // hps, hps * FOX_HEAD
    return pl.pallas_call(
        _fox_attn_body,
        grid=(bsz, hg, s // tq),
        in_specs=[
            pl.BlockSpec((1, tq, w), lambda b, h, i: (b, i, h)),
            pl.BlockSpec((1, s, w), lambda b, h, i: (b, 0, hg + h)),
            pl.BlockSpec((1, s, w), lambda b, h, i: (b, 0, 2 * hg + h)),
            pl.BlockSpec((1, hps, s, LANES), lambda b, h, i: (b, h, 0, 0)),
        ],
        out_specs=pl.BlockSpec((1, tq, w), lambda b, h, i: (b, i, h)),
        out_shape=jax.ShapeDtypeStruct((bsz, s, heads * FOX_HEAD), BF16),
        scratch_shapes=[pltpu.VMEM((hps, 1, tq), F32), pltpu.VMEM((hps, 1, tq), F32),
                        pltpu.VMEM((hps, FOX_HEAD, tq), F32)],
        compiler_params=_params("parallel", "parallel", "arbitrary"),
        name="fox_attn",
    )(p, p, p, cp)


def _gla_intra_t(q, k, b):
    L = q.shape[0]
    sub = min(GLA_SUB, L)
    key = lax.broadcasted_iota(jnp.int32, (L, L), 0)
    qry = lax.broadcasted_iota(jnp.int32, (L, L), 1)
    row = lax.broadcasted_iota(jnp.int32, (L, 1), 0)

    a_t = jnp.zeros((L, L), F32)
    if L > sub:
        ks, qs = [], []
        for blk in range(1, L // sub):
            r = b[blk * sub - 1:blk * sub, :]
            ks.append((k * jnp.exp(jnp.minimum(r - b, 0.0))).astype(BF16))
            in_blk = jnp.logical_and(row >= blk * sub, row < (blk + 1) * sub)
            qs.append((q * jnp.exp(jnp.where(in_blk, b - r, NEG_BIG))).astype(BF16))
        a_off = _dot_nt(jnp.concatenate(ks, axis=1), jnp.concatenate(qs, axis=1))
        a_t = jnp.where(key < lax.bitwise_and(qry, jnp.int32(-sub)), a_off, 0.0)

    a_diag = jnp.zeros((L, L), F32)
    for t in range(L):
        lo, hi = (t // sub) * sub, 8 * (t // 8 + 1)
        s_idx = lo + lax.broadcasted_iota(jnp.int32, (hi - lo, 1), 0)
        diff = jnp.where(s_idx <= t, b[t:t + 1, :] - b[lo:hi, :], NEG_BIG)
        w = jnp.exp(diff) * (k[lo:hi, :] * q[t:t + 1, :])
        colsum = jnp.sum(w, axis=1, keepdims=True)
        pieces = [jnp.zeros((n, 1), F32) for n in (lo,) if n] + [colsum] + [jnp.zeros((n, 1), F32) for n in (L - hi,) if n]
        a_diag = jnp.where(qry == t, jnp.concatenate(pieces, axis=0), a_diag)
    return a_t + a_diag


def _gla_body(q_ref, k_ref, v_ref, r_ref, alow_ref, wal_ref, bal_ref, hn_ref, y_ref, st_ref, *, heads):
    L = q_ref.shape[1]
    dk, dv = GLA_K, GLA_V
    qscale = dk ** -0.5

    @pl.when(pl.program_id(1) == 0)
    def _():
        st_ref[...] = jnp.zeros_like(st_ref)

    log_alpha = jax.nn.log_sigmoid(_dot_f32(alow_ref[0], wal_ref[...]) + bal_ref[...]) * (1.0 / GLA_GATE_TEMP)
    b_all = _cumsum_rows(log_alpha)

    for h in range(heads):
        b = b_all[:, h * dk:(h + 1) * dk]
        q = q_ref[0, :, h * dk:(h + 1) * dk].astype(F32) * qscale
        k = k_ref[0, :, h * dk:(h + 1) * dk].astype(F32)
        v = v_ref[0, :, h * dv:(h + 1) * dv]
        a_t = _gla_intra_t(q, k, b)
        st = st_ref[h]
        o = _dot_tn(a_t.astype(BF16), v) + _dot_nt((q * jnp.exp(b)).astype(BF16), st.astype(BF16))
        b_last = b[L - 1:L, :]
        k_dec = (k * jnp.exp(b_last - b)).astype(BF16)
        st_ref[h] = jnp.exp(b_last) * st + _dot_tn(v, k_dec)
        o = _rmsnorm(o, hn_ref[:, h * dv:(h + 1) * dv])
        r = r_ref[0, :, h * dv:(h + 1) * dv].astype(F32)
        y_ref[0, :, h * dv:(h + 1) * dv] = (o * (r * jax.nn.sigmoid(r))).astype(BF16)


def _gla_core(p, alow, w_alpha, b_alpha, hn, heads):
    bsz, s, _ = p.shape
    nk, nv = heads * GLA_K, heads * GLA_V
    L = min(GLA_CHUNK, s)
    assert s % L == 0 and nv == 2 * nk
    return pl.pallas_call(
        functools.partial(_gla_body, heads=heads),
        grid=(bsz, s // L),
        in_specs=[
            pl.BlockSpec((1, L, nk), lambda b, c: (b, c, 0)),
            pl.BlockSpec((1, L, nk), lambda b, c: (b, c, 1)),
            pl.BlockSpec((1, L, nv), lambda b, c: (b, c, 1)),
            pl.BlockSpec((1, L, nv), lambda b, c: (b, c, 2)),
            pl.BlockSpec((1, L, LANES), lambda b, c: (b, c, 0)),
            pl.BlockSpec((LANES, nk), lambda b, c: (0, 0)),
            pl.BlockSpec((1, nk), lambda b, c: (0, 0)),
            pl.BlockSpec((1, nv), lambda b, c: (0, 0)),
        ],
        out_specs=pl.BlockSpec((1, L, nv), lambda b, c: (b, c, 0)),
        out_shape=jax.ShapeDtypeStruct((bsz, s, nv), BF16),
        scratch_shapes=[pltpu.VMEM((heads, GLA_V, GLA_K), F32)],
        compiler_params=_params("parallel", "arbitrary"),
        name="gla_core",
    )(p, p, p, p, alow, w_alpha, b_alpha, hn)


def _pad_cols(w, width, offsets):
    out = jnp.zeros((w.shape[0], width), w.dtype)
    c0 = 0
    for off, n in offsets:
        out = out.at[:, off:off + n].set(w[:, c0:c0 + n])
        c0 += n
    return out


def _mlstm_layer(h, bsz, norm_g, w_in, b_if, head_norm, w_out):
    t, d = h.shape
    heads = d // MLSTM_V
    n_main = 2 * heads * MLSTM_QK + 2 * heads * MLSTM_V
    wg = _pad_cols(w_in[:, n_main:], 2 * LANES, [(0, heads), (LANES, heads)])
    bias = _pad_cols(b_if.reshape(1, -1), 2 * LANES, [(0, heads), (LANES, heads)])
    p, gates = _proj(h, norm_g, w_in[:, :n_main].astype(BF16), wg.astype(BF16))
    y = _mlstm_core(p.reshape(bsz, t // bsz, n_main), gates.reshape(bsz, t // bsz, 2 * LANES), bias,
                    head_norm.reshape(1, -1), heads)
    return _out_proj(h, y.reshape(t, -1), w_out.astype(BF16))


def _fox_layer(h, bsz, norm_g, w_in, b_f, w_out):
    t, d = h.shape
    heads = d // FOX_HEAD
    n_main = 3 * heads * FOX_HEAD
    wg = _pad_cols(w_in[:, n_main:], LANES, [(0, heads)])
    bias = _pad_cols(b_f.reshape(1, -1), LANES, [(0, heads)])
    p, gates = _proj(h, norm_g, w_in[:, :n_main].astype(BF16), wg.astype(BF16))
    s = t // bsz
    cp = _fox_gates(gates.reshape(bsz, s, LANES), bias, heads)
    y = _fox_attn(p.reshape(bsz, s, n_main), cp, heads)
    return _out_proj(h, y.reshape(t, -1), w_out.astype(BF16))


def _gla_layer(h, bsz, norm_g, w_in, w_alpha, b_alpha, head_norm, w_out):
    t, d = h.shape
    heads = d // GLA_V
    n_main = 2 * heads * GLA_K + 2 * heads * GLA_V
    rank = w_alpha.shape[0]
    wg = _pad_cols(w_in[:, n_main:], LANES, [(0, rank)])
    wal = jnp.zeros((LANES, w_alpha.shape[1]), F32).at[:rank].set(w_alpha)
    p, alow = _proj(h, norm_g, w_in[:, :n_main].astype(BF16), wg.astype(BF16))
    s = t // bsz
    y = _gla_core(p.reshape(bsz, s, n_main), alow.reshape(bsz, s, LANES), wal, b_alpha.reshape(1, -1),
                  head_norm.reshape(1, -1), heads)
    return _out_proj(h, y.reshape(t, -1), w_out.astype(BF16))


def kernel(x, ffn_norm, ffn_w_in, ffn_w_out, mix_norm, a_w_in, a_b_if, a_head_norm, a_w_out, b_w_in, b_b_f, b_w_out, c_w_in, c_w_alpha, c_b_alpha, c_head_norm, c_w_out, final_norm):
    bsz, s, d = x.shape
    depth = ffn_norm.shape[0]
    h = x.reshape(bsz * s, d)
    for i in range(depth):
        h = _ffn(h, ffn_norm[i, 0], ffn_w_in[i, 0].astype(BF16), ffn_w_out[i, 0].astype(BF16))
        kind, j = i % 3, i // 3
        if kind == 0:
            h = _mlstm_layer(h, bsz, mix_norm[i], a_w_in[j], a_b_if[j], a_head_norm[j], a_w_out[j])
        elif kind == 1:
            h = _fox_layer(h, bsz, mix_norm[i], b_w_in[j], b_b_f[j], b_w_out[j])
        else:
            h = _gla_layer(h, bsz, mix_norm[i], c_w_in[j], c_w_alpha[j], c_b_alpha[j], c_head_norm[j], c_w_out[j])
        h = _ffn(h, ffn_norm[i, 1], ffn_w_in[i, 1].astype(BF16), ffn_w_out[i, 1].astype(BF16),
                 final_g=final_norm if i == depth - 1 else None)
    return h.reshape(bsz, s, d)
```

```python
import functools

import jax
import jax.numpy as jnp
from jax import lax
from jax.experimental import pallas as pl
from jax.experimental.pallas import tpu as pltpu

F32 = jnp.float32
BF16 = jnp.bfloat16

RMS_EPS = 1e-6
FFN_RESIDUAL_WEIGHT = 0.5
GLA_GATE_TEMP = 16.0
NEG_BIG = -1e30

LANES = 128
MLSTM_QK, MLSTM_V = 128, 256
FOX_HEAD = 128
GLA_K, GLA_V = 256, 512

VMEM_LIMIT_BYTES = 60 * 1024 * 1024

FFN_TM, FFN_TF, FFN_ROWS = 1024, 512, 512
PROJ_TM, PROJ_TN, PROJ_ROWS = 1024, 1536, 512
OUT_TM = 512
MLSTM_CHUNK = 256
FOX_TQ = 512
FOX_HEADS_PER_STEP = 4
FOX_GATE_CHUNK = 512
GLA_CHUNK = 128
GLA_SUB = 16


def _params(*sem):
    return pltpu.CompilerParams(dimension_semantics=sem, vmem_limit_bytes=VMEM_LIMIT_BYTES)


def _tile(n, pref):
    if n <= pref:
        return n
    return max(c for c in range(LANES, pref + 1, LANES) if n % c == 0)


def _dot(a, b):
    return jnp.dot(a, b, preferred_element_type=F32)


def _dot_nt(a, b):
    return lax.dot_general(a, b, (((1,), (1,)), ((), ())), preferred_element_type=F32)


def _dot_tn(a, b):
    return lax.dot_general(a, b, (((0,), (0,)), ((), ())), preferred_element_type=F32)


def _split2(x):
    hi = x.astype(BF16)
    lo = (x - hi.astype(F32)).astype(BF16)
    return hi, lo


def _split3(x):
    hi = x.astype(BF16)
    r = x - hi.astype(F32)
    mid = r.astype(BF16)
    lo = (r - mid.astype(F32)).astype(BF16)
    return hi, mid, lo


def _dot_f32(a, b):
    ah, al = _split2(a)
    bh, bl = _split2(b)
    return _dot(ah, bh) + (_dot(ah, bl) + _dot(al, bh))


def _cumsum_rows(x):
    n = x.shape[0]
    tril = (lax.broadcasted_iota(jnp.int32, (n, n), 0) >= lax.broadcasted_iota(jnp.int32, (n, n), 1)).astype(BF16)
    hi, mid, lo = _split3(x)
    return _dot(tril, hi) + (_dot(tril, mid) + _dot(tril, lo))


def _cumsum_lanes(x):
    n = x.shape[1]
    triu = (lax.broadcasted_iota(jnp.int32, (n, n), 0) <= lax.broadcasted_iota(jnp.int32, (n, n), 1)).astype(BF16)
    hi, mid, lo = _split3(x)
    return _dot(hi, triu) + (_dot(mid, triu) + _dot(lo, triu))


def _rmsnorm(x, g):
    return x * lax.rsqrt(jnp.mean(x * x, axis=-1, keepdims=True) + RMS_EPS) * g


def _ffn_body(x_ref, g_ref, wa_ref, wb_ref, wo_ref, fg_ref, o_ref, u_ref, *, nf, final_norm):
    f = pl.program_id(1)
    tm = x_ref.shape[0]
    rs = min(FFN_ROWS, tm)
    row_tiles = [pl.ds(r * rs, rs) for r in range(tm // rs)]

    def step(first, last):
        for rows in row_tiles:
            if first:
                u = _rmsnorm(x_ref[rows, :], g_ref[...]).astype(BF16)
                u_ref[rows, :] = u
            else:
                u = u_ref[rows, :]
            a = _dot(u, wa_ref[...])
            b = _dot(u, wb_ref[...])
            hid = (a * jax.nn.sigmoid(a) * b).astype(BF16)
            acc = _dot(hid, wo_ref[...])
            if not first:
                acc = o_ref[rows, :] + acc
            if last:
                acc = x_ref[rows, :] + FFN_RESIDUAL_WEIGHT * acc
                if final_norm:
                    acc = _rmsnorm(acc, fg_ref[...])
            o_ref[rows, :] = acc

    if nf == 1:
        step(True, True)
    else:
        pl.when(f == 0)(lambda: step(True, False))
        if nf > 2:
            pl.when(jnp.logical_and(f > 0, f < nf - 1))(lambda: step(False, False))
        pl.when(f == nf - 1)(lambda: step(False, True))


def _ffn(h, g, w_in, w_out, final_g=None):
    t, d = h.shape
    dff = w_out.shape[0]
    tm, tf = _tile(t, FFN_TM), _tile(dff, FFN_TF)
    assert t % tm == 0 and dff % tf == 0
    nf = dff // tf
    fg = g if final_g is None else final_g
    return pl.pallas_call(
        functools.partial(_ffn_body, nf=nf, final_norm=final_g is not None),
        grid=(t // tm, nf),
        in_specs=[
            pl.BlockSpec((tm, d), lambda i, f: (i, 0)),
            pl.BlockSpec((1, d), lambda i, f: (0, 0)),
            pl.BlockSpec((d, tf), lambda i, f: (0, f)),
            pl.BlockSpec((d, tf), lambda i, f: (0, f + nf)),
            pl.BlockSpec((tf, d), lambda i, f: (f, 0)),
            pl.BlockSpec((1, d), lambda i, f: (0, 0)),
        ],
        out_specs=pl.BlockSpec((tm, d), lambda i, f: (i, 0)),
        out_shape=jax.ShapeDtypeStruct((t, d), F32),
        scratch_shapes=[pltpu.VMEM((tm, d), BF16)],
        compiler_params=_params("parallel", "arbitrary"),
        name="ffn",
    )(h, g.reshape(1, d), w_in, w_in, w_out, fg.reshape(1, d))


def _proj_body(x_ref, g_ref, w_ref, wg_ref, p_ref, gate_ref, u_ref):
    tm = x_ref.shape[0]
    rs = min(PROJ_ROWS, tm)

    def step(first):
        for r in range(tm // rs):
            rows = pl.ds(r * rs, rs)
            if first:
                u = _rmsnorm(x_ref[rows, :], g_ref[...]).astype(BF16)
                u_ref[rows, :] = u
                gate_ref[rows, :] = _dot(u, wg_ref[...])
            else:
                u = u_ref[rows, :]
            p_ref[rows, :] = _dot(u, w_ref[...]).astype(BF16)

    pl.when(pl.program_id(1) == 0)(lambda: step(True))
    pl.when(pl.program_id(1) > 0)(lambda: step(False))


def _proj(h, g, w, wg):
    t, d = h.shape
    n, gw = w.shape[1], wg.shape[1]
    tm, tn = _tile(t, PROJ_TM), _tile(n, PROJ_TN)
    assert t % tm == 0 and n % tn == 0
    return pl.pallas_call(
        _proj_body,
        grid=(t // tm, n // tn),
        in_specs=[
            pl.BlockSpec((tm, d), lambda i, j: (i, 0)),
            pl.BlockSpec((1, d), lambda i, j: (0, 0)),
            pl.BlockSpec((d, tn), lambda i, j: (0, j)),
            pl.BlockSpec((d, gw), lambda i, j: (0, 0)),
        ],
        out_specs=[
            pl.BlockSpec((tm, tn), lambda i, j: (i, j)),
            pl.BlockSpec((tm, gw), lambda i, j: (i, 0)),
        ],
        out_shape=[jax.ShapeDtypeStruct((t, n), BF16), jax.ShapeDtypeStruct((t, gw), F32)],
        scratch_shapes=[pltpu.VMEM((tm, d), BF16)],
        compiler_params=_params("parallel", "arbitrary"),
        name="mixer_in_proj",
    )(h, g.reshape(1, d), w, wg)


def _out_body(h_ref, y_ref, w_ref, o_ref):
    o_ref[...] = h_ref[...] + _dot(y_ref[...], w_ref[...])


def _out_proj(h, y, w):
    t, d = h.shape
    k = y.shape[1]
    tm = min(OUT_TM, t)
    assert t % tm == 0
    return pl.pallas_call(
        _out_body,
        grid=(t // tm,),
        in_specs=[
            pl.BlockSpec((tm, d), lambda i: (i, 0)),
            pl.BlockSpec((tm, k), lambda i: (i, 0)),
            pl.BlockSpec((k, d), lambda i: (0, 0)),
        ],
        out_specs=pl.BlockSpec((tm, d), lambda i: (i, 0)),
        out_shape=jax.ShapeDtypeStruct((t, d), F32),
        compiler_params=_params("parallel"),
        name="mixer_out_proj",
    )(h, y, w)


def _mlstm_body(q_ref, k_ref, v_ref, o_ref, gate_ref, bias_ref, hn_ref, y_ref, c_ref, m_ref, *, heads):
    L = q_ref.shape[1]
    dqk, dv = MLSTM_QK, MLSTM_V
    scale = dqk ** -0.5

    @pl.when(pl.program_id(1) == 0)
    def _():
        c_ref[...] = jnp.zeros_like(c_ref)
        m_ref[...] = jnp.zeros_like(m_ref)

    gates = gate_ref[0] + bias_ref[...]
    i_pre = gates[:, :LANES]
    b_cum = _cumsum_rows(jax.nn.log_sigmoid(gates[:, LANES:]))
    b_t = b_cum.T
    i_t = i_pre.T
    m_all = m_ref[...]
    row = lax.broadcasted_iota(jnp.int32, (L, L), 0)
    col = lax.broadcasted_iota(jnp.int32, (L, L), 1)
    causal = col <= row
    ones_blk = jnp.ones((L, LANES), BF16)

    def gate_stage(h):
        bcol, icol = b_cum[:, h:h + 1], i_pre[:, h:h + 1]
        brow, irow = b_t[h:h + 1, :], i_t[h:h + 1, :]
        m_prev = m_all[:, h:h + 1]
        d_log = jnp.where(causal, bcol - brow + irow, NEG_BIG)
        inter_log = bcol + m_prev
        m_t = jnp.maximum(inter_log, jnp.max(d_log, axis=1, keepdims=True))
        d_w = jnp.exp(d_log - m_t)
        inter_w = jnp.exp(inter_log - m_t)
        q = q_ref[0, :, h * dqk:(h + 1) * dqk]
        k = k_ref[0, :, h * dqk:(h + 1) * dqk]
        s = (_dot_nt(q, k) * (d_w * scale)).astype(BF16)

        b_last = bcol[L - 1:L, :]
        g = b_last - bcol + icol
        m_new = jnp.maximum(b_last + m_prev, jnp.max(g, axis=0, keepdims=True))
        w_c = jnp.exp(b_last + m_prev - m_new)
        w_s = jnp.exp(g - m_new)
        k_w = (k.astype(F32) * (w_s * scale)).astype(BF16)
        m_ref[:, h:h + 1] = m_new
        return q, s, inter_w, jnp.exp(-m_t), w_c, k_w

    def value_stage(h, q, s, inter_w, floor, w_c, k_w):
        v_ext = jnp.concatenate([v_ref[0, :, h * dv:(h + 1) * dv], ones_blk], axis=1)
        c_prev = c_ref[h]
        nd = _dot(s, v_ext) + inter_w * _dot(q, c_prev.astype(BF16))
        num, den = nd[:, :dv], nd[:, dv:dv + 1]
        hh = num / jnp.maximum(jnp.abs(den), floor)
        hh = _rmsnorm(hh, hn_ref[:, h * dv:(h + 1) * dv])
        gate_o = jax.nn.sigmoid(o_ref[0, :, h * dv:(h + 1) * dv].astype(F32))
        y_ref[0, :, h * dv:(h + 1) * dv] = (hh * gate_o).astype(BF16)
        c_ref[h] = w_c * c_prev + _dot_tn(k_w, v_ext)

    staged = None
    for h in range(heads + 1):
        nxt = gate_stage(h) if h < heads else None
        if staged is not None:
            value_stage(h - 1, *staged)
        staged = nxt


def _mlstm_core(p, gates, bias, hn, heads):
    bsz, s, _ = p.shape
    nq, nv = heads * MLSTM_QK, heads * MLSTM_V
    L = min(MLSTM_CHUNK, s)
    assert s % L == 0 and nv == 2 * nq
    return pl.pallas_call(
        functools.partial(_mlstm_body, heads=heads),
        grid=(bsz, s // L),
        in_specs=[
            pl.BlockSpec((1, L, nq), lambda b, c: (b, c, 0)),
            pl.BlockSpec((1, L, nq), lambda b, c: (b, c, 1)),
            pl.BlockSpec((1, L, nv), lambda b, c: (b, c, 1)),
            pl.BlockSpec((1, L, nv), lambda b, c: (b, c, 2)),
            pl.BlockSpec((1, L, 2 * LANES), lambda b, c: (b, c, 0)),
            pl.BlockSpec((1, 2 * LANES), lambda b, c: (0, 0)),
            pl.BlockSpec((1, nv), lambda b, c: (0, 0)),
        ],
        out_specs=pl.BlockSpec((1, L, nv), lambda b, c: (b, c, 0)),
        out_shape=jax.ShapeDtypeStruct((bsz, s, nv), BF16),
        scratch_shapes=[pltpu.VMEM((heads, MLSTM_QK, MLSTM_V + LANES), F32), pltpu.VMEM((1, LANES), F32)],
        compiler_params=_params("parallel", "arbitrary"),
        name="mlstm_core",
    )(p, p, p, p, gates, bias, hn)


FOX_BIAS_PIECES = 3


def _fox_gate_body(gate_ref, bias_ref, cp_ref, carry_ref, *, heads):
    @pl.when(pl.program_id(1) == 0)
    def _():
        carry_ref[...] = jnp.zeros_like(carry_ref)

    log_f = jax.nn.log_sigmoid(gate_ref[0] + bias_ref[...])
    cum = _cumsum_rows(log_f) + carry_ref[...]
    tc = cum.shape[0]
    carry_ref[...] = cum[tc - 1:tc, :]
    lane = lax.broadcasted_iota(jnp.int32, (tc, LANES), 1)
    inv_scale = -(FOX_HEAD ** 0.5)
    for h in range(heads):
        c = jnp.broadcast_to(cum[:, h:h + 1] * inv_scale, (tc, LANES))
        hi, mid, lo = (x.astype(F32) for x in _split3(c))
        pieces = jnp.where(lane == 0, hi, jnp.where(lane == 1, mid, jnp.where(lane == 2, lo, 0.0)))
        cp_ref[0, h] = pieces.astype(BF16)


def _fox_gates(gates, bias, heads):
    bsz, s, _ = gates.shape
    tc = min(FOX_GATE_CHUNK, s)
    assert s % tc == 0
    return pl.pallas_call(
        functools.partial(_fox_gate_body, heads=heads),
        grid=(bsz, s // tc),
        in_specs=[
            pl.BlockSpec((1, tc, LANES), lambda b, c: (b, c, 0)),
            pl.BlockSpec((1, LANES), lambda b, c: (0, 0)),
        ],
        out_specs=pl.BlockSpec((1, heads, tc, LANES), lambda b, c: (b, 0, c, 0)),
        out_shape=jax.ShapeDtypeStruct((bsz, heads, s, LANES), BF16),
        scratch_shapes=[pltpu.VMEM((1, LANES), F32)],
        compiler_params=_params("parallel", "arbitrary"),
        name="fox_gates",
    )(gates, bias)


def _fox_attn_body(q_ref, k_ref, v_ref, cp_ref, y_ref, m_ref, l_ref, acc_ref):
    tq = q_ref.shape[1]
    hps = q_ref.shape[2] // FOX_HEAD
    qi = pl.program_id(2)
    exp2_scale = (FOX_HEAD ** -0.5) * 1.4426950408889634
    lane = lax.broadcasted_iota(jnp.int32, (tq, LANES), 1)
    ones_cols = jnp.where(lane < FOX_BIAS_PIECES, 1.0, 0.0).astype(BF16)
    heads = [pl.ds(g * FOX_HEAD, FOX_HEAD) for g in range(hps)]
    q_ext = [jnp.concatenate([q_ref[0, :, cols], ones_cols], axis=1) for cols in heads]
    m_ref[...] = jnp.full_like(m_ref, NEG_BIG)
    l_ref[...] = jnp.zeros_like(l_ref)
    acc_ref[...] = jnp.zeros_like(acc_ref)

    def block(start, tk, masked):
        rows = pl.ds(start, tk)
        scores = []
        for g, cols in enumerate(heads):
            k_ext = jnp.concatenate([k_ref[0, rows, cols], cp_ref[0, g, rows, :]], axis=1)
            s = _dot_nt(k_ext, q_ext[g])
            if masked:
                key = lax.broadcasted_iota(jnp.int32, (tk, tq), 0)
                qry = lax.broadcasted_iota(jnp.int32, (tk, tq), 1)
                s = jnp.where(key <= qry, s, NEG_BIG)
            scores.append(s)
        for g, cols in enumerate(heads):
            s = scores[g]
            m_old = m_ref[g]
            m_new = jnp.maximum(m_old, jnp.max(s, axis=0, keepdims=True))
            alpha = jnp.exp2((m_old - m_new) * exp2_scale)
            p = jnp.exp2((s - m_new) * exp2_scale)
            l_ref[g] = alpha * l_ref[g] + jnp.sum(p, axis=0, keepdims=True)
            acc_ref[g] = alpha * acc_ref[g] + _dot_tn(v_ref[0, rows, cols], p.astype(BF16))
            m_ref[g] = m_new

    def pair_body(j, carry):
        block(pl.multiple_of(j * (2 * tq), 2 * tq), 2 * tq, masked=False)
        return carry

    lax.fori_loop(0, lax.shift_right_logical(qi, jnp.int32(1)), pair_body, 0)

    @pl.when(lax.bitwise_and(qi, jnp.int32(1)) == 1)
    def _():
        block(pl.multiple_of((qi - 1) * tq, tq), tq, masked=False)

    block(pl.multiple_of(qi * tq, tq), tq, masked=True)
    for g, cols in enumerate(heads):
        y_ref[0, :, cols] = (acc_ref[g] / l_ref[g]).T.astype(BF16)


def _fox_attn(p, cp, heads):
    bsz, s, _ = p.shape
    tq = min(FOX_TQ, s)
    hps = FOX_HEADS_PER_STEP
    assert s % tq == 0 and heads % hps == 0
    hg, w = heads // hps, hps * FOX_HEAD
    return pl.pallas_call(
        _fox_attn_body,
        grid=(bsz, hg, s // tq),
        in_specs=[
            pl.BlockSpec((1, tq, w), lambda b, h, i: (b, i, h)),
            pl.BlockSpec((1, s, w), lambda b, h, i: (b, 0, hg + h)),
            pl.BlockSpec((1, s, w), lambda b, h, i: (b, 0, 2 * hg + h)),
            pl.BlockSpec((1, hps, s, LANES), lambda b, h, i: (b, h, 0, 0)),
        ],
        out_specs=pl.BlockSpec((1, tq, w), lambda b, h, i: (b, i, h)),
        out_shape=jax.ShapeDtypeStruct((bsz, s, heads * FOX_HEAD), BF16),
        scratch_shapes=[pltpu.VMEM((hps, 1, tq), F32), pltpu.VMEM((hps, 1, tq), F32),
                        pltpu.VMEM((hps, FOX_HEAD, tq), F32)],
        compiler_params=_params("parallel", "parallel", "arbitrary"),
        name="fox_attn",
    )(p, p, p, cp)


def _gla_intra_t(q, k, b):
    L = q.shape[0]
    sub = min(GLA_SUB, L)
    key = lax.broadcasted_iota(jnp.int32, (L, L), 0)
    qry = lax.broadcasted_iota(jnp.int32, (L, L), 1)
    row = lax.broadcasted_iota(jnp.int32, (L, 1), 0)

    a_t = jnp.zeros((L, L), F32)
    if L > sub:
        ks, qs = [], []
        for blk in range(1, L // sub):
            r = b[blk * sub - 1:blk * sub, :]
            ks.append((k * jnp.exp(jnp.minimum(r - b, 0.0))).astype(BF16))
            in_blk = jnp.logical_and(row >= blk * sub, row < (blk + 1) * sub)
            qs.append((q * jnp.exp(jnp.where(in_blk, b - r, NEG_BIG))).astype(BF16))
        a_off = _dot_nt(jnp.concatenate(ks, axis=1), jnp.concatenate(qs, axis=1))
        a_t = jnp.where(key < lax.bitwise_and(qry, jnp.int32(-sub)), a_off, 0.0)

    a_diag = jnp.zeros((L, L), F32)
    for t in range(L):
        lo, hi = (t // sub) * sub, 8 * (t // 8 + 1)
        s_idx = lo + lax.broadcasted_iota(jnp.int32, (hi - lo, 1), 0)
        diff = jnp.where(s_idx <= t, b[t:t + 1, :] - b[lo:hi, :], NEG_BIG)
        w = jnp.exp(diff) * (k[lo:hi, :] * q[t:t + 1, :])
        colsum = jnp.sum(w, axis=1, keepdims=True)
        pieces = [jnp.zeros((n, 1), F32) for n in (lo,) if n] + [colsum] + [jnp.zeros((n, 1), F32) for n in (L - hi,) if n]
        a_diag = jnp.where(qry == t, jnp.concatenate(pieces, axis=0), a_diag)
    return a_t + a_diag


def _gla_body(q_ref, k_ref, v_ref, r_ref, alow_ref, wal_ref, bal_ref, hn_ref, y_ref, st_ref, *, heads):
    L = q_ref.shape[1]
    dk, dv = GLA_K, GLA_V
    qscale = dk ** -0.5

    @pl.when(pl.program_id(1) == 0)
    def _():
        st_ref[...] = jnp.zeros_like(st_ref)

    log_alpha = jax.nn.log_sigmoid(_dot_f32(alow_ref[0], wal_ref[...]) + bal_ref[...]) * (1.0 / GLA_GATE_TEMP)
    b_all = _cumsum_rows(log_alpha)

    def decay_stage(h):
        b = b_all[:, h * dk:(h + 1) * dk]
        q = q_ref[0, :, h * dk:(h + 1) * dk].astype(F32) * qscale
        k = k_ref[0, :, h * dk:(h + 1) * dk].astype(F32)
        a_t = _gla_intra_t(q, k, b).astype(BF16)
        b_last = b[L - 1:L, :]
        q_dec = (q * jnp.exp(b)).astype(BF16)
        k_dec = (k * jnp.exp(b_last - b)).astype(BF16)
        return a_t, q_dec, k_dec, jnp.exp(b_last)

    def value_stage(h, a_t, q_dec, k_dec, chunk_decay):
        v = v_ref[0, :, h * dv:(h + 1) * dv]
        st = st_ref[h]
        o = _dot_tn(a_t, v) + _dot_nt(q_dec, st.astype(BF16))
        st_ref[h] = chunk_decay * st + _dot_tn(v, k_dec)
        o = _rmsnorm(o, hn_ref[:, h * dv:(h + 1) * dv])
        r = r_ref[0, :, h * dv:(h + 1) * dv].astype(F32)
        y_ref[0, :, h * dv:(h + 1) * dv] = (o * (r * jax.nn.sigmoid(r))).astype(BF16)

    staged = None
    for h in range(heads + 1):
        nxt = decay_stage(h) if h < heads else None
        if staged is not None:
            value_stage(h - 1, *staged)
        staged = nxt


def _gla_core(p, alow, w_alpha, b_alpha, hn, heads):
    bsz, s, _ = p.shape
    nk, nv = heads * GLA_K, heads * GLA_V
    L = min(GLA_CHUNK, s)
    assert s % L == 0 and nv == 2 * nk
    return pl.pallas_call(
        functools.partial(_gla_body, heads=heads),
        grid=(bsz, s // L),
        in_specs=[
            pl.BlockSpec((1, L, nk), lambda b, c: (b, c, 0)),
            pl.BlockSpec((1, L, nk), lambda b, c: (b, c, 1)),
            pl.BlockSpec((1, L, nv), lambda b, c: (b, c, 1)),
            pl.BlockSpec((1, L, nv), lambda b, c: (b, c, 2)),
            pl.BlockSpec((1, L, LANES), lambda b, c: (b, c, 0)),
            pl.BlockSpec((LANES, nk), lambda b, c: (0, 0)),
            pl.BlockSpec((1, nk), lambda b, c: (0, 0)),
            pl.BlockSpec((1, nv), lambda b, c: (0, 0)),
        ],
        out_specs=pl.BlockSpec((1, L, nv), lambda b, c: (b, c, 0)),
        out_shape=jax.ShapeDtypeStruct((bsz, s, nv), BF16),
        scratch_shapes=[pltpu.VMEM((heads, GLA_V, GLA_K), F32)],
        compiler_params=_params("parallel", "arbitrary"),
        name="gla_core",
    )(p, p, p, p, alow, w_alpha, b_alpha, hn)


def _pad_cols(w, width, offsets):
    out = jnp.zeros((w.shape[0], width), w.dtype)
    c0 = 0
    for off, n in offsets:
        out = out.at[:, off:off + n].set(w[:, c0:c0 + n])
        c0 += n
    return out


def _mlstm_layer(h, bsz, norm_g, w_in, b_if, head_norm, w_out):
    t, d = h.shape
    heads = d // MLSTM_V
    n_main = 2 * heads * MLSTM_QK + 2 * heads * MLSTM_V
    wg = _pad_cols(w_in[:, n_main:], 2 * LANES, [(0, heads), (LANES, heads)])
    bias = _pad_cols(b_if.reshape(1, -1), 2 * LANES, [(0, heads), (LANES, heads)])
    p, gates = _proj(h, norm_g, w_in[:, :n_main].astype(BF16), wg.astype(BF16))
    y = _mlstm_core(p.reshape(bsz, t // bsz, n_main), gates.reshape(bsz, t // bsz, 2 * LANES), bias,
                    head_norm.reshape(1, -1), heads)
    return _out_proj(h, y.reshape(t, -1), w_out.astype(BF16))


def _fox_layer(h, bsz, norm_g, w_in, b_f, w_out):
    t, d = h.shape
    heads = d // FOX_HEAD
    n_main = 3 * heads * FOX_HEAD
    wg = _pad_cols(w_in[:, n_main:], LANES, [(0, heads)])
    bias = _pad_cols(b_f.reshape(1, -1), LANES, [(0, heads)])
    p, gates = _proj(h, norm_g, w_in[:, :n_main].astype(BF16), wg.astype(BF16))
    s = t // bsz
    cp = _fox_gates(gates.reshape(bsz, s, LANES), bias, heads)
    y = _fox_attn(p.reshape(bsz, s, n_main), cp, heads)
    return _out_proj(h, y.reshape(t, -1), w_out.astype(BF16))


def _gla_layer(h, bsz, norm_g, w_in, w_alpha, b_alpha, head_norm, w_out):
    t, d = h.shape
    heads = d // GLA_V
    n_main = 2 * heads * GLA_K + 2 * heads * GLA_V
    rank = w_alpha.shape[0]
    wg = _pad_cols(w_in[:, n_main:], LANES, [(0, rank)])
    wal = jnp.zeros((LANES, w_alpha.shape[1]), F32).at[:rank].set(w_alpha)
    p, alow = _proj(h, norm_g, w_in[:, :n_main].astype(BF16), wg.astype(BF16))
    s = t // bsz
    y = _gla_core(p.reshape(bsz, s, n_main), alow.reshape(bsz, s, LANES), wal, b_alpha.reshape(1, -1),
                  head_norm.reshape(1, -1), heads)
    return _out_proj(h, y.reshape(t, -1), w_out.astype(BF16))


def kernel(x, ffn_norm, ffn_w_in, ffn_w_out, mix_norm, a_w_in, a_b_if, a_head_norm, a_w_out, b_w_in, b_b_f, b_w_out, c_w_in, c_w_alpha, c_b_alpha, c_head_norm, c_w_out, final_norm):
    bsz, s, d = x.shape
    depth = ffn_norm.shape[0]
    h = x.reshape(bsz * s, d)
    for i in range(depth):
        h = _ffn(h, ffn_norm[i, 0], ffn_w_in[i, 0].astype(BF16), ffn_w_out[i, 0].astype(BF16))
        kind, j = i % 3, i // 3
        if kind == 0:
            h = _mlstm_layer(h, bsz, mix_norm[i], a_w_in[j], a_b_if[j], a_head_norm[j], a_w_out[j])
        elif kind == 1:
            h = _fox_layer(h, bsz, mix_norm[i], b_w_in[j], b_b_f[j], b_w_out[j])
        else:
            h = _gla_layer(h, bsz, mix_norm[i], c_w_in[j], c_w_alpha[j], c_b_alpha[j], c_head_norm[j], c_w_out[j])
        h = _ffn(h, ffn_norm[i, 1], ffn_w_in[i, 1].astype(BF16), ffn_w_out[i, 1].astype(BF16),
                 final_g=final_norm if i == depth - 1 else None)
    return h.reshape(bsz, s, d)
```

```python
import functools

import jax
import jax.numpy as jnp
from jax import lax
from jax.experimental import pallas as pl
from jax.experimental.pallas import tpu as pltpu

F32 = jnp.float32
BF16 = jnp.bfloat16

RMS_EPS = 1e-6
FFN_RESIDUAL_WEIGHT = 0.5
GLA_GATE_TEMP = 16.0
NEG_BIG = -1e30

LANES = 128
MLSTM_QK, MLSTM_V = 128, 256
FOX_HEAD = 128
GLA_K, GLA_V = 256, 512

VMEM_LIMIT_BYTES = 60 * 1024 * 1024

FFN_TM, FFN_TF, FFN_ROWS = 1024, 512, 512
PROJ_TM, PROJ_TN, PROJ_ROWS = 1024, 1536, 512
OUT_TM = 512
MLSTM_CHUNK = 256
FOX_TQ = 512
FOX_HEADS_PER_STEP = 4
FOX_GATE_CHUNK = 512
GLA_CHUNK = 128
GLA_SUB = 16


def _params(*sem):
    return pltpu.CompilerParams(dimension_semantics=sem, vmem_limit_bytes=VMEM_LIMIT_BYTES)


def _tile(n, pref):
    if n <= pref:
        return n
    return max(c for c in range(LANES, pref + 1, LANES) if n % c == 0)


def _dot(a, b):
    return jnp.dot(a, b, preferred_element_type=F32)


def _dot_nt(a, b):
    return lax.dot_general(a, b, (((1,), (1,)), ((), ())), preferred_element_type=F32)


def _dot_tn(a, b):
    return lax.dot_general(a, b, (((0,), (0,)), ((), ())), preferred_element_type=F32)


def _split2(x):
    hi = x.astype(BF16)
    lo = (x - hi.astype(F32)).astype(BF16)
    return hi, lo


def _split3(x):
    hi = x.astype(BF16)
    r = x - hi.astype(F32)
    mid = r.astype(BF16)
    lo = (r - mid.astype(F32)).astype(BF16)
    return hi, mid, lo


def _dot_f32(a, b):
    ah, al = _split2(a)
    bh, bl = _split2(b)
    return _dot(ah, bh) + (_dot(ah, bl) + _dot(al, bh))


def _cumsum_rows(x):
    n = x.shape[0]
    tril = (lax.broadcasted_iota(jnp.int32, (n, n), 0) >= lax.broadcasted_iota(jnp.int32, (n, n), 1)).astype(BF16)
    hi, mid, lo = _split3(x)
    return _dot(tril, hi) + (_dot(tril, mid) + _dot(tril, lo))


def _cumsum_lanes(x):
    n = x.shape[1]
    triu = (lax.broadcasted_iota(jnp.int32, (n, n), 0) <= lax.broadcasted_iota(jnp.int32, (n, n), 1)).astype(BF16)
    hi, mid, lo = _split3(x)
    return _dot(hi, triu) + (_dot(mid, triu) + _dot(lo, triu))


def _rmsnorm(x, g):
    return x * lax.rsqrt(jnp.mean(x * x, axis=-1, keepdims=True) + RMS_EPS) * g


def _ffn_body(x_ref, g_ref, wa_ref, wb_ref, wo_ref, fg_ref, o_ref, u_ref, *, nf, final_norm):
    f = pl.program_id(1)
    tm = x_ref.shape[0]
    rs = min(FFN_ROWS, tm)
    row_tiles = [pl.ds(r * rs, rs) for r in range(tm // rs)]

    def step(first, last):
        for rows in row_tiles:
            if first:
                u = _rmsnorm(x_ref[rows, :], g_ref[...]).astype(BF16)
                u_ref[rows, :] = u
            else:
                u = u_ref[rows, :]
            a = _dot(u, wa_ref[...])
            b = _dot(u, wb_ref[...])
            hid = (a * jax.nn.sigmoid(a) * b).astype(BF16)
            acc = _dot(hid, wo_ref[...])
            if not first:
                acc = o_ref[rows, :] + acc
            if last:
                acc = x_ref[rows, :] + FFN_RESIDUAL_WEIGHT * acc
                if final_norm:
                    acc = _rmsnorm(acc, fg_ref[...])
            o_ref[rows, :] = acc

    if nf == 1:
        step(True, True)
    else:
        pl.when(f == 0)(lambda: step(True, False))
        if nf > 2:
            pl.when(jnp.logical_and(f > 0, f < nf - 1))(lambda: step(False, False))
        pl.when(f == nf - 1)(lambda: step(False, True))


def _ffn(h, g, w_in, w_out, final_g=None):
    t, d = h.shape
    dff = w_out.shape[0]
    tm, tf = _tile(t, FFN_TM), _tile(dff, FFN_TF)
    assert t % tm == 0 and dff % tf == 0
    nf = dff // tf
    fg = g if final_g is None else final_g
    return pl.pallas_call(
        functools.partial(_ffn_body, nf=nf, final_norm=final_g is not None),
        grid=(t // tm, nf),
        in_specs=[
            pl.BlockSpec((tm, d), lambda i, f: (i, 0)),
            pl.BlockSpec((1, d), lambda i, f: (0, 0)),
            pl.BlockSpec((d, tf), lambda i, f: (0, f)),
            pl.BlockSpec((d, tf), lambda i, f: (0, f + nf)),
            pl.BlockSpec((tf, d), lambda i, f: (f, 0)),
            pl.BlockSpec((1, d), lambda i, f: (0, 0)),
        ],
        out_specs=pl.BlockSpec((tm, d), lambda i, f: (i, 0)),
        out_shape=jax.ShapeDtypeStruct((t, d), F32),
        scratch_shapes=[pltpu.VMEM((tm, d), BF16)],
        compiler_params=_params("parallel", "arbitrary"),
        name="ffn",
    )(h, g.reshape(1, d), w_in, w_in, w_out, fg.reshape(1, d))


def _proj_body(x_ref, g_ref, w_ref, wg_ref, p_ref, gate_ref, u_ref):
    tm = x_ref.shape[0]
    rs = min(PROJ_ROWS, tm)

    def step(first):
        for r in range(tm // rs):
            rows = pl.ds(r * rs, rs)
            if first:
                u = _rmsnorm(x_ref[rows, :], g_ref[...]).astype(BF16)
                u_ref[rows, :] = u
                gate_ref[rows, :] = _dot(u, wg_ref[...])
            else:
                u = u_ref[rows, :]
            p_ref[rows, :] = _dot(u, w_ref[...]).astype(BF16)

    pl.when(pl.program_id(1) == 0)(lambda: step(True))
    pl.when(pl.program_id(1) > 0)(lambda: step(False))


def _proj(h, g, w, wg):
    t, d = h.shape
    n, gw = w.shape[1], wg.shape[1]
    tm, tn = _tile(t, PROJ_TM), _tile(n, PROJ_TN)
    assert t % tm == 0 and n % tn == 0
    return pl.pallas_call(
        _proj_body,
        grid=(t // tm, n // tn),
        in_specs=[
            pl.BlockSpec((tm, d), lambda i, j: (i, 0)),
            pl.BlockSpec((1, d), lambda i, j: (0, 0)),
            pl.BlockSpec((d, tn), lambda i, j: (0, j)),
            pl.BlockSpec((d, gw), lambda i, j: (0, 0)),
        ],
        out_specs=[
            pl.BlockSpec((tm, tn), lambda i, j: (i, j)),
            pl.BlockSpec((tm, gw), lambda i, j: (i, 0)),
        ],
        out_shape=[jax.ShapeDtypeStruct((t, n), BF16), jax.ShapeDtypeStruct((t, gw), F32)],
        scratch_shapes=[pltpu.VMEM((tm, d), BF16)],
        compiler_params=_params("parallel", "arbitrary"),
        name="mixer_in_proj",
    )(h, g.reshape(1, d), w, wg)


def _out_body(h_ref, y_ref, w_ref, o_ref):
    o_ref[...] = h_ref[...] + _dot(y_ref[...], w_ref[...])


def _out_proj(h, y, w):
    t, d = h.shape
    k = y.shape[1]
    tm = min(OUT_TM, t)
    assert t % tm == 0
    return pl.pallas_call(
        _out_body,
        grid=(t // tm,),
        in_specs=[
            pl.BlockSpec((tm, d), lambda i: (i, 0)),
            pl.BlockSpec((tm, k), lambda i: (i, 0)),
            pl.BlockSpec((k, d), lambda i: (0, 0)),
        ],
        out_specs=pl.BlockSpec((tm, d), lambda i: (i, 0)),
        out_shape=jax.ShapeDtypeStruct((t, d), F32),
        compiler_params=_params("parallel"),
        name="mixer_out_proj",
    )(h, y, w)


def _mlstm_body(q_ref, k_ref, v_ref, o_ref, gate_ref, bias_ref, hn_ref, y_ref, c_ref, m_ref, *, heads):
    L = q_ref.shape[1]
    dqk, dv = MLSTM_QK, MLSTM_V
    scale = dqk ** -0.5

    @pl.when(pl.program_id(1) == 0)
    def _():
        c_ref[...] = jnp.zeros_like(c_ref)
        m_ref[...] = jnp.zeros_like(m_ref)

    gates = gate_ref[0] + bias_ref[...]
    i_pre = gates[:, :LANES]
    b_cum = _cumsum_rows(jax.nn.log_sigmoid(gates[:, LANES:]))
    b_t = b_cum.T
    i_t = i_pre.T
    m_all = m_ref[...]
    row = lax.broadcasted_iota(jnp.int32, (L, L), 0)
    col = lax.broadcasted_iota(jnp.int32, (L, L), 1)
    causal = col <= row
    ones_blk = jnp.ones((L, LANES), BF16)

    def gate_stage(h):
        bcol, icol = b_cum[:, h:h + 1], i_pre[:, h:h + 1]
        brow, irow = b_t[h:h + 1, :], i_t[h:h + 1, :]
        m_prev = m_all[:, h:h + 1]
        d_log = jnp.where(causal, bcol - brow + irow, NEG_BIG)
        inter_log = bcol + m_prev
        m_t = jnp.maximum(inter_log, jnp.max(d_log, axis=1, keepdims=True))
        d_w = jnp.exp(d_log - m_t)
        inter_w = jnp.exp(inter_log - m_t)
        q = q_ref[0, :, h * dqk:(h + 1) * dqk]
        k = k_ref[0, :, h * dqk:(h + 1) * dqk]
        s = (_dot_nt(q, k) * (d_w * scale)).astype(BF16)

        b_last = bcol[L - 1:L, :]
        g = b_last - bcol + icol
        m_new = jnp.maximum(b_last + m_prev, jnp.max(g, axis=0, keepdims=True))
        w_c = jnp.exp(b_last + m_prev - m_new)
        w_s = jnp.exp(g - m_new)
        k_w = (k.astype(F32) * (w_s * scale)).astype(BF16)
        m_ref[:, h:h + 1] = m_new
        return q, s, inter_w, jnp.exp(-m_t), w_c, k_w

    def value_stage(h, q, s, inter_w, floor, w_c, k_w):
        v_ext = jnp.concatenate([v_ref[0, :, h * dv:(h + 1) * dv], ones_blk], axis=1)
        c_prev = c_ref[h]
        nd = _dot(s, v_ext) + inter_w * _dot(q, c_prev.astype(BF16))
        num, den = nd[:, :dv], nd[:, dv:dv + 1]
        hh = num / jnp.maximum(jnp.abs(den), floor)
        hh = _rmsnorm(hh, hn_ref[:, h * dv:(h + 1) * dv])
        gate_o = jax.nn.sigmoid(o_ref[0, :, h * dv:(h + 1) * dv].astype(F32))
        y_ref[0, :, h * dv:(h + 1) * dv] = (hh * gate_o).astype(BF16)
        c_ref[h] = w_c * c_prev + _dot_tn(k_w, v_ext)

    staged = None
    for h in range(heads + 1):
        nxt = gate_stage(h) if h < heads else None
        if staged is not None:
            value_stage(h - 1, *staged)
        staged = nxt


def _mlstm_core(p, gates, bias, hn, heads):
    bsz, s, _ = p.shape
    nq, nv = heads * MLSTM_QK, heads * MLSTM_V
    L = min(MLSTM_CHUNK, s)
    assert s % L == 0 and nv == 2 * nq
    return pl.pallas_call(
        functools.partial(_mlstm_body, heads=heads),
        grid=(bsz, s // L),
        in_specs=[
            pl.BlockSpec((1, L, nq), lambda b, c: (b, c, 0)),
            pl.BlockSpec((1, L, nq), lambda b, c: (b, c, 1)),
            pl.BlockSpec((1, L, nv), lambda b, c: (b, c, 1)),
            pl.BlockSpec((1, L, nv), lambda b, c: (b, c, 2)),
            pl.BlockSpec((1, L, 2 * LANES), lambda b, c: (b, c, 0)),
            pl.BlockSpec((1, 2 * LANES), lambda b, c: (0, 0)),
            pl.BlockSpec((1, nv), lambda b, c: (0, 0)),
        ],
        out_specs=pl.BlockSpec((1, L, nv), lambda b, c: (b, c, 0)),
        out_shape=jax.ShapeDtypeStruct((bsz, s, nv), BF16),
        scratch_shapes=[pltpu.VMEM((heads, MLSTM_QK, MLSTM_V + LANES), F32), pltpu.VMEM((1, LANES), F32)],
        compiler_params=_params("parallel", "arbitrary"),
        name="mlstm_core",
    )(p, p, p, p, gates, bias, hn)


FOX_BIAS_PIECES = 3


def _fox_gate_body(gate_ref, bias_ref, cp_ref, carry_ref, *, heads):
    @pl.when(pl.program_id(1) == 0)
    def _():
        carry_ref[...] = jnp.zeros_like(carry_ref)

    log_f = jax.nn.log_sigmoid(gate_ref[0] + bias_ref[...])
    cum = _cumsum_rows(log_f) + carry_ref[...]
    tc = cum.shape[0]
    carry_ref[...] = cum[tc - 1:tc, :]
    lane = lax.broadcasted_iota(jnp.int32, (tc, LANES), 1)
    inv_scale = -(FOX_HEAD ** 0.5)
    for h in range(heads):
        c = jnp.broadcast_to(cum[:, h:h + 1] * inv_scale, (tc, LANES))
        hi, mid, lo = (x.astype(F32) for x in _split3(c))
        pieces = jnp.where(lane == 0, hi, jnp.where(lane == 1, mid, jnp.where(lane == 2, lo, 0.0)))
        cp_ref[0, h] = pieces.astype(BF16)


def _fox_gates(gates, bias, heads):
    bsz, s, _ = gates.shape
    tc = min(FOX_GATE_CHUNK, s)
    assert s % tc == 0
    return pl.pallas_call(
        functools.partial(_fox_gate_body, heads=heads),
        grid=(bsz, s // tc),
        in_specs=[
            pl.BlockSpec((1, tc, LANES), lambda b, c: (b, c, 0)),
            pl.BlockSpec((1, LANES), lambda b, c: (0, 0)),
        ],
        out_specs=pl.BlockSpec((1, heads, tc, LANES), lambda b, c: (b, 0, c, 0)),
        out_shape=jax.ShapeDtypeStruct((bsz, heads, s, LANES), BF16),
        scratch_shapes=[pltpu.VMEM((1, LANES), F32)],
        compiler_params=_params("parallel", "arbitrary"),
        name="fox_gates",
    )(gates, bias)


def _fox_attn_body(q_ref, k_ref, v_ref, cp_ref, y_ref, sa_ref, sb_ref, m_ref, l_ref, acc_ref):
    tq = q_ref.shape[1]
    hps = q_ref.shape[2] // FOX_HEAD
    qi = pl.program_id(2)
    exp2_scale = (FOX_HEAD ** -0.5) * 1.4426950408889634
    lane = lax.broadcasted_iota(jnp.int32, (tq, LANES), 1)
    ones_cols = jnp.where(lane < FOX_BIAS_PIECES, 1.0, 0.0).astype(BF16)
    heads = [pl.ds(g * FOX_HEAD, FOX_HEAD) for g in range(hps)]
    q_ext = [jnp.concatenate([q_ref[0, :, cols], ones_cols], axis=1) for cols in heads]
    m_ref[...] = jnp.full_like(m_ref, NEG_BIG)
    l_ref[...] = jnp.zeros_like(l_ref)
    acc_ref[...] = jnp.zeros_like(acc_ref)

    def scores(j, dst_ref, masked):
        rows = pl.ds(pl.multiple_of(j * tq, tq), tq)
        for g, cols in enumerate(heads):
            k_ext = jnp.concatenate([k_ref[0, rows, cols], cp_ref[0, g, rows, :]], axis=1)
            s = _dot_nt(k_ext, q_ext[g])
            if masked:
                key = lax.broadcasted_iota(jnp.int32, (tq, tq), 0)
                qry = lax.broadcasted_iota(jnp.int32, (tq, tq), 1)
                s = jnp.where(key <= qry, s, NEG_BIG)
            dst_ref[g] = s

    def consume(j, src_ref):
        rows = pl.ds(pl.multiple_of(j * tq, tq), tq)
        for g, cols in enumerate(heads):
            m_old = m_ref[g]
            m_new = jnp.maximum(m_old, jnp.max(src_ref[g], axis=0, keepdims=True))
            alpha = jnp.exp2((m_old - m_new) * exp2_scale)
            p = jnp.exp2((src_ref[g] - m_new) * exp2_scale)
            l_ref[g] = alpha * l_ref[g] + jnp.sum(p, axis=0, keepdims=True)
            acc_ref[g] = alpha * acc_ref[g] + _dot_tn(v_ref[0, rows, cols], p.astype(BF16))
            m_ref[g] = m_new

    one = jnp.int32(1)
    pl.when(qi == 0)(lambda: scores(0, sa_ref, True))
    pl.when(qi > 0)(lambda: scores(0, sa_ref, False))
    n_pairs = lax.shift_right_logical(jnp.maximum(qi - 1, 0), one)

    def pair_body(i, carry):
        j = 2 * i
        scores(j + 1, sb_ref, False)
        consume(j, sa_ref)
        scores(j + 2, sa_ref, False)
        consume(j + 1, sb_ref)
        return carry

    lax.fori_loop(0, n_pairs, pair_body, 0)

    @pl.when(qi == 0)
    def _():
        consume(0, sa_ref)

    @pl.when(lax.bitwise_and(qi, one) == 1)
    def _():
        scores(qi, sb_ref, True)
        consume(qi - 1, sa_ref)
        consume(qi, sb_ref)

    @pl.when(jnp.logical_and(qi > 0, lax.bitwise_and(qi, one) == 0))
    def _():
        scores(qi - 1, sb_ref, False)
        consume(qi - 2, sa_ref)
        scores(qi, sa_ref, True)
        consume(qi - 1, sb_ref)
        consume(qi, sa_ref)

    for g, cols in enumerate(heads):
        y_ref[0, :, cols] = (acc_ref[g] / l_ref[g]).T.astype(BF16)


def _fox_attn(p, cp, heads):
    bsz, s, _ = p.shape
    tq = min(FOX_TQ, s)
    hps = FOX_HEADS_PER_STEP
    assert s % tq == 0 and heads % hps == 0
    hg, w = heads // hps, hps * FOX_HEAD
    return pl.pallas_call(
        _fox_attn_body,
        grid=(bsz, hg, s // tq),
        in_specs=[
            pl.BlockSpec((1, tq, w), lambda b, h, i: (b, i, h)),
            pl.BlockSpec((1, s, w), lambda b, h, i: (b, 0, hg + h)),
            pl.BlockSpec((1, s, w), lambda b, h, i: (b, 0, 2 * hg + h)),
            pl.BlockSpec((1, hps, s, LANES), lambda b, h, i: (b, h, 0, 0)),
        ],
        out_specs=pl.BlockSpec((1, tq, w), lambda b, h, i: (b, i, h)),
        out_shape=jax.ShapeDtypeStruct((bsz, s, heads * FOX_HEAD), BF16),
        scratch_shapes=[pltpu.VMEM((hps, tq, tq), F32), pltpu.VMEM((hps, tq, tq), F32),
                        pltpu.VMEM((hps, 1, tq), F32), pltpu.VMEM((hps, 1, tq), F32),
                        pltpu.VMEM((hps, FOX_HEAD, tq), F32)],
        compiler_params=_params("parallel", "parallel", "arbitrary"),
        name="fox_attn",
    )(p, p, p, cp)


def _gla_intra_t(q, k, b):
    L = q.shape[0]
    sub = min(GLA_SUB, L)

    a_t = jnp.zeros((L, L), F32)
    if L > sub:
        dk = q.shape[1]
        zeros = lambda n: [jnp.zeros((n, dk), BF16)] if n else []
        ks, qs = [], []
        for n in range(sub, L, sub):
            r = b[n - 1:n, :]
            k_part = (k[:n, :] * jnp.exp(r - b[:n, :])).astype(BF16)
            q_part = (q[n:n + sub, :] * jnp.exp(b[n:n + sub, :] - r)).astype(BF16)
            ks.append(jnp.concatenate([k_part] + zeros(L - n), axis=0))
            qs.append(jnp.concatenate(zeros(n) + [q_part] + zeros(L - n - sub), axis=0))
        a_t = _dot_nt(jnp.concatenate(ks, axis=1), jnp.concatenate(qs, axis=1))

    strip_t = lax.broadcasted_iota(jnp.int32, (sub, L), 1)
    strips = []
    for lo in range(0, L, sub):
        strip = jnp.zeros((sub, L), F32)
        for t in range(lo, lo + sub):
            hi = 8 * (t // 8 + 1)
            s_idx = lo + lax.broadcasted_iota(jnp.int32, (hi - lo, 1), 0)
            diff = jnp.where(s_idx <= t, b[t:t + 1, :] - b[lo:hi, :], NEG_BIG)
            w = jnp.exp(diff) * (k[lo:hi, :] * q[t:t + 1, :])
            colsum = jnp.sum(w, axis=1, keepdims=True)
            if hi < lo + sub:
                colsum = jnp.concatenate([colsum, jnp.zeros((lo + sub - hi, 1), F32)], axis=0)
            strip = jnp.where(strip_t == t, colsum, strip)
        strips.append(strip)
    return a_t + jnp.concatenate(strips, axis=0)


def _gla_body(q_ref, k_ref, v_ref, r_ref, alow_ref, wal_ref, bal_ref, hn_ref, y_ref, st_ref, *, heads):
    L = q_ref.shape[1]
    dk, dv = GLA_K, GLA_V
    qscale = dk ** -0.5

    @pl.when(pl.program_id(1) == 0)
    def _():
        st_ref[...] = jnp.zeros_like(st_ref)

    log_alpha = jax.nn.log_sigmoid(_dot_f32(alow_ref[0], wal_ref[...]) + bal_ref[...]) * (1.0 / GLA_GATE_TEMP)
    b_all = _cumsum_rows(log_alpha)

    def decay_stage(h):
        b = b_all[:, h * dk:(h + 1) * dk]
        q = q_ref[0, :, h * dk:(h + 1) * dk].astype(F32) * qscale
        k = k_ref[0, :, h * dk:(h + 1) * dk].astype(F32)
        a_t = _gla_intra_t(q, k, b).astype(BF16)
        b_last = b[L - 1:L, :]
        q_dec = (q * jnp.exp(b)).astype(BF16)
        k_dec = (k * jnp.exp(b_last - b)).astype(BF16)
        return a_t, q_dec, k_dec, jnp.exp(b_last)

    def value_stage(h, a_t, q_dec, k_dec, chunk_decay):
        v = v_ref[0, :, h * dv:(h + 1) * dv]
        st = st_ref[h]
        o = _dot_tn(a_t, v) + _dot_nt(q_dec, st.astype(BF16))
        st_ref[h] = chunk_decay * st + _dot_tn(v, k_dec)
        o = _rmsnorm(o, hn_ref[:, h * dv:(h + 1) * dv])
        r = r_ref[0, :, h * dv:(h + 1) * dv].astype(F32)
        y_ref[0, :, h * dv:(h + 1) * dv] = (o * (r * jax.nn.sigmoid(r))).astype(BF16)

    staged = None
    for h in range(heads + 1):
        nxt = decay_stage(h) if h < heads else None
        if staged is not None:
            value_stage(h - 1, *staged)
        staged = nxt


def _gla_core(p, alow, w_alpha, b_alpha, hn, heads):
    bsz, s, _ = p.shape
    nk, nv = heads * GLA_K, heads * GLA_V
    L = min(GLA_CHUNK, s)
    assert s % L == 0 and nv == 2 * nk
    return pl.pallas_call(
        functools.partial(_gla_body, heads=heads),
        grid=(bsz, s // L),
        in_specs=[
            pl.BlockSpec((1, L, nk), lambda b, c: (b, c, 0)),
            pl.BlockSpec((1, L, nk), lambda b, c: (b, c, 1)),
            pl.BlockSpec((1, L, nv), lambda b, c: (b, c, 1)),
            pl.BlockSpec((1, L, nv), lambda b, c: (b, c, 2)),
            pl.BlockSpec((1, L, LANES), lambda b, c: (b, c, 0)),
            pl.BlockSpec((LANES, nk), lambda b, c: (0, 0)),
            pl.BlockSpec((1, nk), lambda b, c: (0, 0)),
            pl.BlockSpec((1, nv), lambda b, c: (0, 0)),
        ],
        out_specs=pl.BlockSpec((1, L, nv), lambda b, c: (b, c, 0)),
        out_shape=jax.ShapeDtypeStruct((bsz, s, nv), BF16),
        scratch_shapes=[pltpu.VMEM((heads, GLA_V, GLA_K), F32)],
        compiler_params=_params("parallel", "arbitrary"),
        name="gla_core",
    )(p, p, p, p, alow, w_alpha, b_alpha, hn)


def _pad_cols(w, width, offsets):
    out = jnp.zeros((w.shape[0], width), w.dtype)
    c0 = 0
    for off, n in offsets:
        out = out.at[:, off:off + n].set(w[:, c0:c0 + n])
        c0 += n
    return out


def _mlstm_layer(h, bsz, norm_g, w_in, b_if, head_norm, w_out):
    t, d = h.shape
    heads = d // MLSTM_V
    n_main = 2 * heads * MLSTM_QK + 2 * heads * MLSTM_V
    wg = _pad_cols(w_in[:, n_main:], 2 * LANES, [(0, heads), (LANES, heads)])
    bias = _pad_cols(b_if.reshape(1, -1), 2 * LANES, [(0, heads), (LANES, heads)])
    p, gates = _proj(h, norm_g, w_in[:, :n_main].astype(BF16), wg.astype(BF16))
    y = _mlstm_core(p.reshape(bsz, t // bsz, n_main), gates.reshape(bsz, t // bsz, 2 * LANES), bias,
                    head_norm.reshape(1, -1), heads)
    return _out_proj(h, y.reshape(t, -1), w_out.astype(BF16))


def _fox_layer(h, bsz, norm_g, w_in, b_f, w_out):
    t, d = h.shape
    heads = d // FOX_HEAD
    n_main = 3 * heads * FOX_HEAD
    wg = _pad_cols(w_in[:, n_main:], LANES, [(0, heads)])
    bias = _pad_cols(b_f.reshape(1, -1), LANES, [(0, heads)])
    p, gates = _proj(h, norm_g, w_in[:, :n_main].astype(BF16), wg.astype(BF16))
    s = t // bsz
    cp = _fox_gates(gates.reshape(bsz, s, LANES), bias, heads)
    y = _fox_attn(p.reshape(bsz, s, n_main), cp, heads)
    return _out_proj(h, y.reshape(t, -1), w_out.astype(BF16))


def _gla_layer(h, bsz, norm_g, w_in, w_alpha, b_alpha, head_norm, w_out):
    t, d = h.shape
    heads = d // GLA_V
    n_main = 2 * heads * GLA_K + 2 * heads * GLA_V
    rank = w_alpha.shape[0]
    wg = _pad_cols(w_in[:, n_main:], LANES, [(0, rank)])
    wal = jnp.zeros((LANES, w_alpha.shape[1]), F32).at[:rank].set(w_alpha)
    p, alow = _proj(h, norm_g, w_in[:, :n_main].astype(BF16), wg.astype(BF16))
    s = t // bsz
    y = _gla_core(p.reshape(bsz, s, n_main), alow.reshape(bsz, s, LANES), wal, b_alpha.reshape(1, -1),
                  head_norm.reshape(1, -1), heads)
    return _out_proj(h, y.reshape(t, -1), w_out.astype(BF16))


def kernel(x, ffn_norm, ffn_w_in, ffn_w_out, mix_norm, a_w_in, a_b_if, a_head_norm, a_w_out, b_w_in, b_b_f, b_w_out, c_w_in, c_w_alpha, c_b_alpha, c_head_norm, c_w_out, final_norm):
    bsz, s, d = x.shape
    depth = ffn_norm.shape[0]
    h = x.reshape(bsz * s, d)
    for i in range(depth):
        h = _ffn(h, ffn_norm[i, 0], ffn_w_in[i, 0].astype(BF16), ffn_w_out[i, 0].astype(BF16))
        kind, j = i % 3, i // 3
        if kind == 0:
            h = _mlstm_layer(h, bsz, mix_norm[i], a_w_in[j], a_b_if[j], a_head_norm[j], a_w_out[j])
        elif kind == 1:
            h = _fox_layer(h, bsz, mix_norm[i], b_w_in[j], b_b_f[j], b_w_out[j])
        else:
            h = _gla_layer(h, bsz, mix_norm[i], c_w_in[j], c_w_alpha[j], c_b_alpha[j], c_head_norm[j], c_w_out[j])
        h = _ffn(h, ffn_norm[i, 1], ffn_w_in[i, 1].astype(BF16), ffn_w_out[i, 1].astype(BF16),
                 final_g=final_norm if i == depth - 1 else None)
    return h.reshape(bsz, s, d)
```

```python
import functools

import jax
import jax.numpy as jnp
from jax import lax
from jax.experimental import pallas as pl
from jax.experimental.pallas import tpu as pltpu

F32 = jnp.float32
BF16 = jnp.bfloat16

RMS_EPS = 1e-6
FFN_RESIDUAL_WEIGHT = 0.5
GLA_GATE_TEMP = 16.0
NEG_BIG = -1e30

LANES = 128
MLSTM_QK, MLSTM_V = 128, 256
FOX_HEAD = 128
GLA_K, GLA_V = 256, 512

VMEM_LIMIT_BYTES = 60 * 1024 * 1024

FFN_TM, FFN_TF, FFN_ROWS = 1024, 512, 512
PROJ_TM, PROJ_TN, PROJ_ROWS = 1024, 1536, 512
OUT_TM = 512
MLSTM_CHUNK = 256
FOX_TQ = 512
FOX_HEADS_PER_STEP = 4
FOX_GATE_CHUNK = 512
GLA_CHUNK = 128
GLA_SUB = 16


def _params(*sem):
    return pltpu.CompilerParams(dimension_semantics=sem, vmem_limit_bytes=VMEM_LIMIT_BYTES)


def _tile(n, pref):
    if n <= pref:
        return n
    return max(c for c in range(LANES, pref + 1, LANES) if n % c == 0)


def _dot(a, b):
    return jnp.dot(a, b, preferred_element_type=F32)


def _dot_nt(a, b):
    return lax.dot_general(a, b, (((1,), (1,)), ((), ())), preferred_element_type=F32)


def _dot_tn(a, b):
    return lax.dot_general(a, b, (((0,), (0,)), ((), ())), preferred_element_type=F32)


def _split2(x):
    hi = x.astype(BF16)
    lo = (x - hi.astype(F32)).astype(BF16)
    return hi, lo


def _split3(x):
    hi = x.astype(BF16)
    r = x - hi.astype(F32)
    mid = r.astype(BF16)
    lo = (r - mid.astype(F32)).astype(BF16)
    return hi, mid, lo


def _dot_f32(a, b):
    ah, al = _split2(a)
    bh, bl = _split2(b)
    return _dot(ah, bh) + (_dot(ah, bl) + _dot(al, bh))


def _cumsum_rows(x):
    n = x.shape[0]
    tril = (lax.broadcasted_iota(jnp.int32, (n, n), 0) >= lax.broadcasted_iota(jnp.int32, (n, n), 1)).astype(BF16)
    hi, mid, lo = _split3(x)
    return _dot(tril, hi) + (_dot(tril, mid) + _dot(tril, lo))


def _cumsum_lanes(x):
    n = x.shape[1]
    triu = (lax.broadcasted_iota(jnp.int32, (n, n), 0) <= lax.broadcasted_iota(jnp.int32, (n, n), 1)).astype(BF16)
    hi, mid, lo = _split3(x)
    return _dot(hi, triu) + (_dot(mid, triu) + _dot(lo, triu))


def _rmsnorm(x, g):
    return x * lax.rsqrt(jnp.mean(x * x, axis=-1, keepdims=True) + RMS_EPS) * g


def _ffn_body(x_ref, g_ref, wa_ref, wb_ref, wo_ref, fg_ref, o_ref, u_ref, *, nf, final_norm):
    f = pl.program_id(1)
    tm = x_ref.shape[0]
    rs = min(FFN_ROWS, tm)
    row_tiles = [pl.ds(r * rs, rs) for r in range(tm // rs)]

    def step(first, last):
        for rows in row_tiles:
            if first:
                u = _rmsnorm(x_ref[rows, :], g_ref[...]).astype(BF16)
                u_ref[rows, :] = u
            else:
                u = u_ref[rows, :]
            a = _dot(u, wa_ref[...])
            b = _dot(u, wb_ref[...])
            hid = (a * jax.nn.sigmoid(a) * b).astype(BF16)
            acc = _dot(hid, wo_ref[...])
            if not first:
                acc = o_ref[rows, :] + acc
            if last:
                acc = x_ref[rows, :] + FFN_RESIDUAL_WEIGHT * acc
                if final_norm:
                    acc = _rmsnorm(acc, fg_ref[...])
            o_ref[rows, :] = acc

    if nf == 1:
        step(True, True)
    else:
        pl.when(f == 0)(lambda: step(True, False))
        if nf > 2:
            pl.when(jnp.logical_and(f > 0, f < nf - 1))(lambda: step(False, False))
        pl.when(f == nf - 1)(lambda: step(False, True))


def _ffn(h, g, w_in, w_out, final_g=None):
    t, d = h.shape
    dff = w_out.shape[0]
    tm, tf = _tile(t, FFN_TM), _tile(dff, FFN_TF)
    assert t % tm == 0 and dff % tf == 0
    nf = dff // tf
    fg = g if final_g is None else final_g
    return pl.pallas_call(
        functools.partial(_ffn_body, nf=nf, final_norm=final_g is not None),
        grid=(t // tm, nf),
        in_specs=[
            pl.BlockSpec((tm, d), lambda i, f: (i, 0)),
            pl.BlockSpec((1, d), lambda i, f: (0, 0)),
            pl.BlockSpec((d, tf), lambda i, f: (0, f)),
            pl.BlockSpec((d, tf), lambda i, f: (0, f + nf)),
            pl.BlockSpec((tf, d), lambda i, f: (f, 0)),
            pl.BlockSpec((1, d), lambda i, f: (0, 0)),
        ],
        out_specs=pl.BlockSpec((tm, d), lambda i, f: (i, 0)),
        out_shape=jax.ShapeDtypeStruct((t, d), F32),
        scratch_shapes=[pltpu.VMEM((tm, d), BF16)],
        compiler_params=_params("parallel", "arbitrary"),
        name="ffn",
    )(h, g.reshape(1, d), w_in, w_in, w_out, fg.reshape(1, d))


def _proj_body(x_ref, g_ref, w_ref, wg_ref, p_ref, gate_ref, u_ref):
    tm = x_ref.shape[0]
    rs = min(PROJ_ROWS, tm)

    def step(first):
        for r in range(tm // rs):
            rows = pl.ds(r * rs, rs)
            if first:
                u = _rmsnorm(x_ref[rows, :], g_ref[...]).astype(BF16)
                u_ref[rows, :] = u
                gate_ref[rows, :] = _dot(u, wg_ref[...])
            else:
                u = u_ref[rows, :]
            p_ref[rows, :] = _dot(u, w_ref[...]).astype(BF16)

    pl.when(pl.program_id(1) == 0)(lambda: step(True))
    pl.when(pl.program_id(1) > 0)(lambda: step(False))


def _proj(h, g, w, n, wg):
    t, d = h.shape
    gw = wg.shape[1]
    tm, tn = _tile(t, PROJ_TM), _tile(n, PROJ_TN)
    assert t % tm == 0 and n % tn == 0
    return pl.pallas_call(
        _proj_body,
        grid=(t // tm, n // tn),
        in_specs=[
            pl.BlockSpec((tm, d), lambda i, j: (i, 0)),
            pl.BlockSpec((1, d), lambda i, j: (0, 0)),
            pl.BlockSpec((d, tn), lambda i, j: (0, j)),
            pl.BlockSpec((d, gw), lambda i, j: (0, 0)),
        ],
        out_specs=[
            pl.BlockSpec((tm, tn), lambda i, j: (i, j)),
            pl.BlockSpec((tm, gw), lambda i, j: (i, 0)),
        ],
        out_shape=[jax.ShapeDtypeStruct((t, n), BF16), jax.ShapeDtypeStruct((t, gw), F32)],
        scratch_shapes=[pltpu.VMEM((tm, d), BF16)],
        compiler_params=_params("parallel", "arbitrary"),
        name="mixer_in_proj",
    )(h, g.reshape(1, d), w, wg)


def _out_body(h_ref, y_ref, w_ref, o_ref):
    o_ref[...] = h_ref[...] + _dot(y_ref[...], w_ref[...])


def _out_proj(h, y, w):
    t, d = h.shape
    k = y.shape[1]
    tm = min(OUT_TM, t)
    assert t % tm == 0
    return pl.pallas_call(
        _out_body,
        grid=(t // tm,),
        in_specs=[
            pl.BlockSpec((tm, d), lambda i: (i, 0)),
            pl.BlockSpec((tm, k), lambda i: (i, 0)),
            pl.BlockSpec((k, d), lambda i: (0, 0)),
        ],
        out_specs=pl.BlockSpec((tm, d), lambda i: (i, 0)),
        out_shape=jax.ShapeDtypeStruct((t, d), F32),
        compiler_params=_params("parallel"),
        name="mixer_out_proj",
    )(h, y, w)


def _mlstm_body(q_ref, k_ref, v_ref, o_ref, gate_ref, bias_ref, hn_ref, y_ref, c_ref, m_ref, *, heads):
    L = q_ref.shape[1]
    dqk, dv = MLSTM_QK, MLSTM_V
    scale = dqk ** -0.5

    @pl.when(pl.program_id(1) == 0)
    def _():
        c_ref[...] = jnp.zeros_like(c_ref)
        m_ref[...] = jnp.zeros_like(m_ref)

    gates = gate_ref[0] + bias_ref[...]
    i_pre = gates[:, :LANES]
    b_cum = _cumsum_rows(jax.nn.log_sigmoid(gates[:, LANES:]))
    m_all = m_ref[...]
    row = lax.broadcasted_iota(jnp.int32, (L, L), 0)
    col = lax.broadcasted_iota(jnp.int32, (L, L), 1)
    causal = col <= row
    ones_blk = jnp.ones((L, LANES), BF16)

    inter_log_all = b_cum + m_all
    b_last = b_cum[L - 1:L, :]
    g = b_last - b_cum + i_pre
    m_new = jnp.maximum(b_last + m_all, jnp.max(g, axis=0, keepdims=True))
    w_c_all = jnp.exp(b_last + m_all - m_new)
    w_s_all = jnp.exp(g - m_new) * scale
    m_ref[...] = m_new

    r_all = i_pre - b_cum
    r_t = r_all.T
    time = lax.broadcasted_iota(jnp.int32, (L, LANES), 0)
    run_max, step = r_all, 1
    while step < L:
        run_max = jnp.where(time >= step, jnp.maximum(run_max, pltpu.roll(run_max, step, 0)), run_max)
        step *= 2
    m_t_all = jnp.maximum(inter_log_all, b_cum + run_max)
    c_all = b_cum - m_t_all
    inter_w_all = jnp.exp(inter_log_all - m_t_all)
    floor_all = jnp.exp(-m_t_all)

    def gate_stage(h):
        col = slice(h, h + 1)
        d_w = jnp.where(causal, jnp.exp(c_all[:, col] + r_t[col, :]), 0.0)
        q = q_ref[0, :, h * dqk:(h + 1) * dqk]
        k = k_ref[0, :, h * dqk:(h + 1) * dqk]
        s = (_dot_nt(q, k) * (d_w * scale)).astype(BF16)
        k_w = (k.astype(F32) * w_s_all[:, col]).astype(BF16)
        return q, s, inter_w_all[:, col], floor_all[:, col], w_c_all[:, col], k_w

    def value_stage(h, q, s, inter_w, floor, w_c, k_w):
        v_ext = jnp.concatenate([v_ref[0, :, h * dv:(h + 1) * dv], ones_blk], axis=1)
        c_prev = c_ref[h]
        nd = _dot(s, v_ext) + inter_w * _dot(q, c_prev.astype(BF16))
        num, den = nd[:, :dv], nd[:, dv:dv + 1]
        hh = num / jnp.maximum(jnp.abs(den), floor)
        hh = _rmsnorm(hh, hn_ref[:, h * dv:(h + 1) * dv])
        gate_o = jax.nn.sigmoid(o_ref[0, :, h * dv:(h + 1) * dv].astype(F32))
        y_ref[0, :, h * dv:(h + 1) * dv] = (hh * gate_o).astype(BF16)
        c_ref[h] = w_c * c_prev + _dot_tn(k_w, v_ext)

    staged = None
    for h in range(heads + 1):
        nxt = gate_stage(h) if h < heads else None
        if staged is not None:
            value_stage(h - 1, *staged)
        staged = nxt


def _mlstm_core(p, gates, bias, hn, heads):
    bsz, s, _ = p.shape
    nq, nv = heads * MLSTM_QK, heads * MLSTM_V
    L = min(MLSTM_CHUNK, s)
    assert s % L == 0 and nv == 2 * nq
    return pl.pallas_call(
        functools.partial(_mlstm_body, heads=heads),
        grid=(bsz, s // L),
        in_specs=[
            pl.BlockSpec((1, L, nq), lambda b, c: (b, c, 0)),
            pl.BlockSpec((1, L, nq), lambda b, c: (b, c, 1)),
            pl.BlockSpec((1, L, nv), lambda b, c: (b, c, 1)),
            pl.BlockSpec((1, L, nv), lambda b, c: (b, c, 2)),
            pl.BlockSpec((1, L, 2 * LANES), lambda b, c: (b, c, 0)),
            pl.BlockSpec((1, 2 * LANES), lambda b, c: (0, 0)),
            pl.BlockSpec((1, nv), lambda b, c: (0, 0)),
        ],
        out_specs=pl.BlockSpec((1, L, nv), lambda b, c: (b, c, 0)),
        out_shape=jax.ShapeDtypeStruct((bsz, s, nv), BF16),
        scratch_shapes=[pltpu.VMEM((heads, MLSTM_QK, MLSTM_V + LANES), F32), pltpu.VMEM((1, LANES), F32)],
        compiler_params=_params("parallel", "arbitrary"),
        name="mlstm_core",
    )(p, p, p, p, gates, bias, hn)


FOX_BIAS_PIECES = 3


def _fox_gate_body(gate_ref, bias_ref, cp_ref, carry_ref, *, heads):
    @pl.when(pl.program_id(1) == 0)
    def _():
        carry_ref[...] = jnp.zeros_like(carry_ref)

    log_f = jax.nn.log_sigmoid(gate_ref[0] + bias_ref[...])
    cum = _cumsum_rows(log_f) + carry_ref[...]
    tc = cum.shape[0]
    carry_ref[...] = cum[tc - 1:tc, :]
    lane = lax.broadcasted_iota(jnp.int32, (tc, LANES), 1)
    inv_scale = -(FOX_HEAD ** 0.5)
    for h in range(heads):
        c = jnp.broadcast_to(cum[:, h:h + 1] * inv_scale, (tc, LANES))
        hi, mid, lo = (x.astype(F32) for x in _split3(c))
        pieces = jnp.where(lane == 0, hi, jnp.where(lane == 1, mid, jnp.where(lane == 2, lo, 0.0)))
        cp_ref[0, h] = pieces.astype(BF16)


def _fox_gates(gates, bias, heads):
    bsz, s, _ = gates.shape
    tc = min(FOX_GATE_CHUNK, s)
    assert s % tc == 0
    return pl.pallas_call(
        functools.partial(_fox_gate_body, heads=heads),
        grid=(bsz, s // tc),
        in_specs=[
            pl.BlockSpec((1, tc, LANES), lambda b, c: (b, c, 0)),
            pl.BlockSpec((1, LANES), lambda b, c: (0, 0)),
        ],
        out_specs=pl.BlockSpec((1, heads, tc, LANES), lambda b, c: (b, 0, c, 0)),
        out_shape=jax.ShapeDtypeStruct((bsz, heads, s, LANES), BF16),
        scratch_shapes=[pltpu.VMEM((1, LANES), F32)],
        compiler_params=_params("parallel", "arbitrary"),
        name="fox_gates",
    )(gates, bias)


def _fox_attn_body(q_ref, k_ref, v_ref, cp_ref, y_ref, sa_ref, sb_ref, m_ref, l_ref, acc_ref):
    tq = q_ref.shape[1]
    hps = q_ref.shape[2] // FOX_HEAD
    qi = pl.program_id(2)
    exp2_scale = (FOX_HEAD ** -0.5) * 1.4426950408889634
    lane = lax.broadcasted_iota(jnp.int32, (tq, LANES), 1)
    ones_cols = jnp.where(lane < FOX_BIAS_PIECES, 1.0, 0.0).astype(BF16)
    heads = [pl.ds(g * FOX_HEAD, FOX_HEAD) for g in range(hps)]
    q_ext = [jnp.concatenate([q_ref[0, :, cols], ones_cols], axis=1) for cols in heads]
    m_ref[...] = jnp.full_like(m_ref, NEG_BIG)
    l_ref[...] = jnp.zeros_like(l_ref)
    acc_ref[...] = jnp.zeros_like(acc_ref)

    def scores(j, dst_ref, masked):
        rows = pl.ds(pl.multiple_of(j * tq, tq), tq)
        for g, cols in enumerate(heads):
            k_ext = jnp.concatenate([k_ref[0, rows, cols], cp_ref[0, g, rows, :]], axis=1)
            s = _dot_nt(k_ext, q_ext[g])
            if masked:
                key = lax.broadcasted_iota(jnp.int32, (tq, tq), 0)
                qry = lax.broadcasted_iota(jnp.int32, (tq, tq), 1)
                s = jnp.where(key <= qry, s, NEG_BIG)
            dst_ref[g] = s

    def consume(j, src_ref):
        rows = pl.ds(pl.multiple_of(j * tq, tq), tq)
        for g, cols in enumerate(heads):
            m_old = m_ref[g]
            m_new = jnp.maximum(m_old, jnp.max(src_ref[g], axis=0, keepdims=True))
            alpha = jnp.exp2((m_old - m_new) * exp2_scale)
            p = jnp.exp2((src_ref[g] - m_new) * exp2_scale)
            l_ref[g] = alpha * l_ref[g] + jnp.sum(p, axis=0, keepdims=True)
            acc_ref[g] = alpha * acc_ref[g] + _dot_tn(v_ref[0, rows, cols], p.astype(BF16))
            m_ref[g] = m_new

    one = jnp.int32(1)
    pl.when(qi == 0)(lambda: scores(0, sa_ref, True))
    pl.when(qi > 0)(lambda: scores(0, sa_ref, False))
    n_pairs = lax.shift_right_logical(jnp.maximum(qi - 1, 0), one)

    def pair_body(i, carry):
        j = 2 * i
        scores(j + 1, sb_ref, False)
        consume(j, sa_ref)
        scores(j + 2, sa_ref, False)
        consume(j + 1, sb_ref)
        return carry

    lax.fori_loop(0, n_pairs, pair_body, 0)

    @pl.when(qi == 0)
    def _():
        consume(0, sa_ref)

    @pl.when(lax.bitwise_and(qi, one) == 1)
    def _():
        scores(qi, sb_ref, True)
        consume(qi - 1, sa_ref)
        consume(qi, sb_ref)

    @pl.when(jnp.logical_and(qi > 0, lax.bitwise_and(qi, one) == 0))
    def _():
        scores(qi - 1, sb_ref, False)
        consume(qi - 2, sa_ref)
        scores(qi, sa_ref, True)
        consume(qi - 1, sb_ref)
        consume(qi, sa_ref)

    for g, cols in enumerate(heads):
        y_ref[0, :, cols] = (acc_ref[g] / l_ref[g]).T.astype(BF16)


def _fox_attn(p, cp, heads):
    bsz, s, _ = p.shape
    tq = min(FOX_TQ, s)
    hps = FOX_HEADS_PER_STEP
    assert s % tq == 0 and heads % hps == 0
    hg, w = heads // hps, hps * FOX_HEAD
    return pl.pallas_call(
        _fox_attn_body,
        grid=(bsz, hg, s // tq),
        in_specs=[
            pl.BlockSpec((1, tq, w), lambda b, h, i: (b, i, h)),
            pl.BlockSpec((1, s, w), lambda b, h, i: (b, 0, hg + h)),
            pl.BlockSpec((1, s, w), lambda b, h, i: (b, 0, 2 * hg + h)),
            pl.BlockSpec((1, hps, s, LANES), lambda b, h, i: (b, h, 0, 0)),
        ],
        out_specs=pl.BlockSpec((1, tq, w), lambda b, h, i: (b, i, h)),
        out_shape=jax.ShapeDtypeStruct((bsz, s, heads * FOX_HEAD), BF16),
        scratch_shapes=[pltpu.VMEM((hps, tq, tq), F32), pltpu.VMEM((hps, tq, tq), F32),
                        pltpu.VMEM((hps, 1, tq), F32), pltpu.VMEM((hps, 1, tq), F32),
                        pltpu.VMEM((hps, FOX_HEAD, tq), F32)],
        compiler_params=_params("parallel", "parallel", "arbitrary"),
        name="fox_attn",
    )(p, p, p, cp)


def _gla_intra_t(q, k, b):
    L = q.shape[0]
    sub = min(GLA_SUB, L)

    a_t = jnp.zeros((L, L), F32)
    if L > sub:
        dk = q.shape[1]
        zeros = lambda n: [jnp.zeros((n, dk), BF16)] if n else []
        ks, qs = [], []
        for n in range(sub, L, sub):
            r = b[n - 1:n, :]
            k_part = (k[:n, :] * jnp.exp(r - b[:n, :])).astype(BF16)
            q_part = (q[n:n + sub, :] * jnp.exp(b[n:n + sub, :] - r)).astype(BF16)
            ks.append(jnp.concatenate([k_part] + zeros(L - n), axis=0))
            qs.append(jnp.concatenate(zeros(n) + [q_part] + zeros(L - n - sub), axis=0))
        a_t = _dot_nt(jnp.concatenate(ks, axis=1), jnp.concatenate(qs, axis=1))

    strip_t = lax.broadcasted_iota(jnp.int32, (sub, L), 1)
    strips = []
    for lo in range(0, L, sub):
        strip = jnp.zeros((sub, L), F32)
        for t in range(lo, lo + sub):
            hi = 8 * (t // 8 + 1)
            s_idx = lo + lax.broadcasted_iota(jnp.int32, (hi - lo, 1), 0)
            diff = jnp.where(s_idx <= t, b[t:t + 1, :] - b[lo:hi, :], NEG_BIG)
            w = jnp.exp(diff) * (k[lo:hi, :] * q[t:t + 1, :])
            colsum = jnp.sum(w, axis=1, keepdims=True)
            if hi < lo + sub:
                colsum = jnp.concatenate([colsum, jnp.zeros((lo + sub - hi, 1), F32)], axis=0)
            strip = jnp.where(strip_t == t, colsum, strip)
        strips.append(strip)
    return a_t + jnp.concatenate(strips, axis=0)


def _gla_body(q_ref, k_ref, v_ref, r_ref, alow_ref, wal_ref, bal_ref, hn_ref, y_ref, st_ref, *, heads):
    L = q_ref.shape[1]
    dk, dv = GLA_K, GLA_V
    qscale = dk ** -0.5

    @pl.when(pl.program_id(1) == 0)
    def _():
        st_ref[...] = jnp.zeros_like(st_ref)

    log_alpha = jax.nn.log_sigmoid(_dot_f32(alow_ref[0], wal_ref[...]) + bal_ref[...]) * (1.0 / GLA_GATE_TEMP)
    b_all = _cumsum_rows(log_alpha)

    def decay_stage(h):
        b = b_all[:, h * dk:(h + 1) * dk]
        q = q_ref[0, :, h * dk:(h + 1) * dk].astype(F32) * qscale
        k = k_ref[0, :, h * dk:(h + 1) * dk].astype(F32)
        a_t = _gla_intra_t(q, k, b).astype(BF16)
        b_last = b[L - 1:L, :]
        q_dec = (q * jnp.exp(b)).astype(BF16)
        k_dec = (k * jnp.exp(b_last - b)).astype(BF16)
        return a_t, q_dec, k_dec, jnp.exp(b_last)

    def value_stage(h, a_t, q_dec, k_dec, chunk_decay):
        v = v_ref[0, :, h * dv:(h + 1) * dv]
        st = st_ref[h]
        o = _dot_tn(a_t, v) + _dot_nt(q_dec, st.astype(BF16))
        st_ref[h] = chunk_decay * st + _dot_tn(v, k_dec)
        o = _rmsnorm(o, hn_ref[:, h * dv:(h + 1) * dv])
        r = r_ref[0, :, h * dv:(h + 1) * dv].astype(F32)
        y_ref[0, :, h * dv:(h + 1) * dv] = (o * (r * jax.nn.sigmoid(r))).astype(BF16)

    staged = None
    for h in range(heads + 1):
        nxt = decay_stage(h) if h < heads else None
        if staged is not None:
            value_stage(h - 1, *staged)
        staged = nxt


def _gla_core(p, alow, w_alpha, b_alpha, hn, heads):
    bsz, s, _ = p.shape
    nk, nv = heads * GLA_K, heads * GLA_V
    L = min(GLA_CHUNK, s)
    assert s % L == 0 and nv == 2 * nk
    return pl.pallas_call(
        functools.partial(_gla_body, heads=heads),
        grid=(bsz, s // L),
        in_specs=[
            pl.BlockSpec((1, L, nk), lambda b, c: (b, c, 0)),
            pl.BlockSpec((1, L, nk), lambda b, c: (b, c, 1)),
            pl.BlockSpec((1, L, nv), lambda b, c: (b, c, 1)),
            pl.BlockSpec((1, L, nv), lambda b, c: (b, c, 2)),
            pl.BlockSpec((1, L, LANES), lambda b, c: (b, c, 0)),
            pl.BlockSpec((LANES, nk), lambda b, c: (0, 0)),
            pl.BlockSpec((1, nk), lambda b, c: (0, 0)),
            pl.BlockSpec((1, nv), lambda b, c: (0, 0)),
        ],
        out_specs=pl.BlockSpec((1, L, nv), lambda b, c: (b, c, 0)),
        out_shape=jax.ShapeDtypeStruct((bsz, s, nv), BF16),
        scratch_shapes=[pltpu.VMEM((heads, GLA_V, GLA_K), F32)],
        compiler_params=_params("parallel", "arbitrary"),
        name="gla_core",
    )(p, p, p, p, alow, w_alpha, b_alpha, hn)


def _pad_cols(w, width, offsets):
    out = jnp.zeros((w.shape[0], width), w.dtype)
    c0 = 0
    for off, n in offsets:
        out = out.at[:, off:off + n].set(w[:, c0:c0 + n])
        c0 += n
    return out


def _mlstm_layer(h, bsz, norm_g, w_in, b_if, head_norm, w_out):
    t, d = h.shape
    heads = d // MLSTM_V
    n_main = 2 * heads * MLSTM_QK + 2 * heads * MLSTM_V
    wg = _pad_cols(w_in[:, n_main:], 2 * LANES, [(0, heads), (LANES, heads)])
    bias = _pad_cols(b_if.reshape(1, -1), 2 * LANES, [(0, heads), (LANES, heads)])
    p, gates = _proj(h, norm_g, w_in.astype(BF16), n_main, wg.astype(BF16))
    y = _mlstm_core(p.reshape(bsz, t // bsz, n_main), gates.reshape(bsz, t // bsz, 2 * LANES), bias,
                    head_norm.reshape(1, -1), heads)
    return _out_proj(h, y.reshape(t, -1), w_out.astype(BF16))


def _fox_layer(h, bsz, norm_g, w_in, b_f, w_out):
    t, d = h.shape
    heads = d // FOX_HEAD
    n_main = 3 * heads * FOX_HEAD
    wg = _pad_cols(w_in[:, n_main:], LANES, [(0, heads)])
    bias = _pad_cols(b_f.reshape(1, -1), LANES, [(0, heads)])
    p, gates = _proj(h, norm_g, w_in.astype(BF16), n_main, wg.astype(BF16))
    s = t // bsz
    cp = _fox_gates(gates.reshape(bsz, s, LANES), bias, heads)
    y = _fox_attn(p.reshape(bsz, s, n_main), cp, heads)
    return _out_proj(h, y.reshape(t, -1), w_out.astype(BF16))


def _gla_layer(h, bsz, norm_g, w_in, w_alpha, b_alpha, head_norm, w_out):
    t, d = h.shape
    heads = d // GLA_V
    n_main = 2 * heads * GLA_K + 2 * heads * GLA_V
    rank = w_alpha.shape[0]
    wg = _pad_cols(w_in[:, n_main:], LANES, [(0, rank)])
    wal = jnp.zeros((LANES, w_alpha.shape[1]), F32).at[:rank].set(w_alpha)
    p, alow = _proj(h, norm_g, w_in.astype(BF16), n_main, wg.astype(BF16))
    s = t // bsz
    y = _gla_core(p.reshape(bsz, s, n_main), alow.reshape(bsz, s, LANES), wal, b_alpha.reshape(1, -1),
                  head_norm.reshape(1, -1), heads)
    return _out_proj(h, y.reshape(t, -1), w_out.astype(BF16))


def kernel(x, ffn_norm, ffn_w_in, ffn_w_out, mix_norm, a_w_in, a_b_if, a_head_norm, a_w_out, b_w_in, b_b_f, b_w_out, c_w_in, c_w_alpha, c_b_alpha, c_head_norm, c_w_out, final_norm):
    bsz, s, d = x.shape
    depth = ffn_norm.shape[0]
    h = x.reshape(bsz * s, d)
    for i in range(depth):
        h = _ffn(h, ffn_norm[i, 0], ffn_w_in[i, 0].astype(BF16), ffn_w_out[i, 0].astype(BF16))
        kind, j = i % 3, i // 3
        if kind == 0:
            h = _mlstm_layer(h, bsz, mix_norm[i], a_w_in[j], a_b_if[j], a_head_norm[j], a_w_out[j])
        elif kind == 1:
            h = _fox_layer(h, bsz, mix_norm[i], b_w_in[j], b_b_f[j], b_w_out[j])
        else:
            h = _gla_layer(h, bsz, mix_norm[i], c_w_in[j], c_w_alpha[j], c_b_alpha[j], c_head_norm[j], c_w_out[j])
        h = _ffn(h, ffn_norm[i, 1], ffn_w_in[i, 1].astype(BF16), ffn_w_out[i, 1].astype(BF16),
                 final_g=final_norm if i == depth - 1 else None)
    return h.reshape(bsz, s, d)
```

```python
import functools

import jax
import jax.numpy as jnp
from jax import lax
from jax.experimental import pallas as pl
from jax.experimental.pallas import tpu as pltpu

F32 = jnp.float32
BF16 = jnp.bfloat16

RMS_EPS = 1e-6
FFN_RESIDUAL_WEIGHT = 0.5
GLA_GATE_TEMP = 16.0
NEG_BIG = -1e30

LANES = 128
MLSTM_QK, MLSTM_V = 128, 256
FOX_HEAD = 128
GLA_K, GLA_V = 256, 512

VMEM_LIMIT_BYTES = 60 * 1024 * 1024

FFN_TM, FFN_TF, FFN_ROWS = 1024, 512, 512
PROJ_TM, PROJ_TN, PROJ_ROWS = 1024, 1536, 512
OUT_TM = 512
MLSTM_CHUNK = 256
FOX_TQ = 512
FOX_HEADS_PER_STEP = 4
FOX_GATE_CHUNK = 512
GLA_CHUNK = 128
GLA_SUB = 16


def _params(*sem):
    return pltpu.CompilerParams(dimension_semantics=sem, vmem_limit_bytes=VMEM_LIMIT_BYTES)


def _tile(n, pref):
    if n <= pref:
        return n
    return max(c for c in range(LANES, pref + 1, LANES) if n % c == 0)


def _dot(a, b):
    return jnp.dot(a, b, preferred_element_type=F32)


def _dot_nt(a, b):
    return lax.dot_general(a, b, (((1,), (1,)), ((), ())), preferred_element_type=F32)


def _dot_tn(a, b):
    return lax.dot_general(a, b, (((0,), (0,)), ((), ())), preferred_element_type=F32)


def _split2(x):
    hi = x.astype(BF16)
    lo = (x - hi.astype(F32)).astype(BF16)
    return hi, lo


def _split3(x):
    hi = x.astype(BF16)
    r = x - hi.astype(F32)
    mid = r.astype(BF16)
    lo = (r - mid.astype(F32)).astype(BF16)
    return hi, mid, lo


def _dot_f32(a, b):
    ah, al = _split2(a)
    bh, bl = _split2(b)
    return _dot(ah, bh) + (_dot(ah, bl) + _dot(al, bh))


def _cumsum_rows(x):
    n = x.shape[0]
    tril = (lax.broadcasted_iota(jnp.int32, (n, n), 0) >= lax.broadcasted_iota(jnp.int32, (n, n), 1)).astype(BF16)
    hi, mid, lo = _split3(x)
    return _dot(tril, hi) + (_dot(tril, mid) + _dot(tril, lo))


def _cumsum_lanes(x):
    n = x.shape[1]
    triu = (lax.broadcasted_iota(jnp.int32, (n, n), 0) <= lax.broadcasted_iota(jnp.int32, (n, n), 1)).astype(BF16)
    hi, mid, lo = _split3(x)
    return _dot(hi, triu) + (_dot(mid, triu) + _dot(lo, triu))


def _rmsnorm(x, g):
    return x * lax.rsqrt(jnp.mean(x * x, axis=-1, keepdims=True) + RMS_EPS) * g


def _ffn_body(x_ref, g_ref, wa_ref, wb_ref, wo_ref, fg_ref, o_ref, u_ref, *, nf, final_norm):
    f = pl.program_id(1)
    tm = x_ref.shape[0]
    rs = min(FFN_ROWS, tm)
    row_tiles = [pl.ds(r * rs, rs) for r in range(tm // rs)]

    def step(first, last):
        for rows in row_tiles:
            if first:
                u = _rmsnorm(x_ref[rows, :], g_ref[...]).astype(BF16)
                u_ref[rows, :] = u
            else:
                u = u_ref[rows, :]
            a = _dot(u, wa_ref[...])
            b = _dot(u, wb_ref[...])
            hid = (a * jax.nn.sigmoid(a) * b).astype(BF16)
            acc = _dot(hid, wo_ref[...])
            if not first:
                acc = o_ref[rows, :] + acc
            if last:
                acc = x_ref[rows, :] + FFN_RESIDUAL_WEIGHT * acc
                if final_norm:
                    acc = _rmsnorm(acc, fg_ref[...])
            o_ref[rows, :] = acc

    if nf == 1:
        step(True, True)
    else:
        pl.when(f == 0)(lambda: step(True, False))
        if nf > 2:
            pl.when(jnp.logical_and(f > 0, f < nf - 1))(lambda: step(False, False))
        pl.when(f == nf - 1)(lambda: step(False, True))


def _ffn(h, g, w_in, w_out, layer, final_g=None):
    t, d = h.shape
    dff = w_out.shape[-2]
    tm, tf = _tile(t, FFN_TM), _tile(dff, FFN_TF)
    assert t % tm == 0 and dff % tf == 0
    nf = dff // tf
    fg = g if final_g is None else final_g
    li, lj = layer
    return pl.pallas_call(
        functools.partial(_ffn_body, nf=nf, final_norm=final_g is not None),
        grid=(t // tm, nf),
        in_specs=[
            pl.BlockSpec((tm, d), lambda i, f: (i, 0)),
            pl.BlockSpec((1, d), lambda i, f: (0, 0)),
            pl.BlockSpec((None, None, d, tf), lambda i, f: (li, lj, 0, f)),
            pl.BlockSpec((None, None, d, tf), lambda i, f: (li, lj, 0, f + nf)),
            pl.BlockSpec((None, None, tf, d), lambda i, f: (li, lj, f, 0)),
            pl.BlockSpec((1, d), lambda i, f: (0, 0)),
        ],
        out_specs=pl.BlockSpec((tm, d), lambda i, f: (i, 0)),
        out_shape=jax.ShapeDtypeStruct((t, d), F32),
        scratch_shapes=[pltpu.VMEM((tm, d), BF16)],
        compiler_params=_params("parallel", "arbitrary"),
        name="ffn",
    )(h, g.reshape(1, d), w_in, w_in, w_out, fg.reshape(1, d))


def _proj_body(x_ref, g_ref, w_ref, wg_ref, p_ref, gate_ref, u_ref):
    tm = x_ref.shape[0]
    rs = min(PROJ_ROWS, tm)

    def step(first):
        for r in range(tm // rs):
            rows = pl.ds(r * rs, rs)
            if first:
                u = _rmsnorm(x_ref[rows, :], g_ref[...]).astype(BF16)
                u_ref[rows, :] = u
                gate_ref[rows, :] = _dot(u, wg_ref[...])
            else:
                u = u_ref[rows, :]
            p_ref[rows, :] = _dot(u, w_ref[...]).astype(BF16)

    pl.when(pl.program_id(1) == 0)(lambda: step(True))
    pl.when(pl.program_id(1) > 0)(lambda: step(False))


def _proj(h, g, w, layer, n, wg):
    t, d = h.shape
    gw = wg.shape[1]
    tm, tn = _tile(t, PROJ_TM), _tile(n, PROJ_TN)
    assert t % tm == 0 and n % tn == 0
    return pl.pallas_call(
        _proj_body,
        grid=(t // tm, n // tn),
        in_specs=[
            pl.BlockSpec((tm, d), lambda i, j: (i, 0)),
            pl.BlockSpec((1, d), lambda i, j: (0, 0)),
            pl.BlockSpec((None, d, tn), lambda i, j: (layer, 0, j)),
            pl.BlockSpec((d, gw), lambda i, j: (0, 0)),
        ],
        out_specs=[
            pl.BlockSpec((tm, tn), lambda i, j: (i, j)),
            pl.BlockSpec((tm, gw), lambda i, j: (i, 0)),
        ],
        out_shape=[jax.ShapeDtypeStruct((t, n), BF16), jax.ShapeDtypeStruct((t, gw), F32)],
        scratch_shapes=[pltpu.VMEM((tm, d), BF16)],
        compiler_params=_params("parallel", "arbitrary"),
        name="mixer_in_proj",
    )(h, g.reshape(1, d), w, wg)


def _out_body(h_ref, y_ref, w_ref, o_ref):
    o_ref[...] = h_ref[...] + _dot(y_ref[...], w_ref[...])


def _out_proj(h, y, w, layer):
    t, d = h.shape
    k = y.shape[1]
    tm = min(OUT_TM, t)
    assert t % tm == 0
    return pl.pallas_call(
        _out_body,
        grid=(t // tm,),
        in_specs=[
            pl.BlockSpec((tm, d), lambda i: (i, 0)),
            pl.BlockSpec((tm, k), lambda i: (i, 0)),
            pl.BlockSpec((None, k, d), lambda i: (layer, 0, 0)),
        ],
        out_specs=pl.BlockSpec((tm, d), lambda i: (i, 0)),
        out_shape=jax.ShapeDtypeStruct((t, d), F32),
        compiler_params=_params("parallel"),
        name="mixer_out_proj",
    )(h, y, w)


def _mlstm_body(q_ref, k_ref, v_ref, o_ref, gate_ref, bias_ref, hn_ref, y_ref, c_ref, m_ref, *, heads):
    L = q_ref.shape[1]
    dqk, dv = MLSTM_QK, MLSTM_V
    scale = dqk ** -0.5

    @pl.when(pl.program_id(1) == 0)
    def _():
        c_ref[...] = jnp.zeros_like(c_ref)
        m_ref[...] = jnp.zeros_like(m_ref)

    gates = gate_ref[0] + bias_ref[...]
    i_pre = gates[:, :LANES]
    b_cum = _cumsum_rows(jax.nn.log_sigmoid(gates[:, LANES:]))
    m_all = m_ref[...]
    row = lax.broadcasted_iota(jnp.int32, (L, L), 0)
    col = lax.broadcasted_iota(jnp.int32, (L, L), 1)
    causal = col <= row
    ones_blk = jnp.ones((L, LANES), BF16)

    inter_log_all = b_cum + m_all
    b_last = b_cum[L - 1:L, :]
    g = b_last - b_cum + i_pre
    m_new = jnp.maximum(b_last + m_all, jnp.max(g, axis=0, keepdims=True))
    w_c_all = jnp.exp(b_last + m_all - m_new)
    w_s_all = jnp.exp(g - m_new) * scale
    m_ref[...] = m_new

    r_all = i_pre - b_cum
    r_t = r_all.T
    time = lax.broadcasted_iota(jnp.int32, (L, LANES), 0)
    run_max, step = r_all, 1
    while step < L:
        run_max = jnp.where(time >= step, jnp.maximum(run_max, pltpu.roll(run_max, step, 0)), run_max)
        step *= 2
    m_t_all = jnp.maximum(inter_log_all, b_cum + run_max)
    c_all = b_cum - m_t_all
    inter_w_all = jnp.exp(inter_log_all - m_t_all)
    floor_all = jnp.exp(-m_t_all)

    def gate_stage(h):
        col = slice(h, h + 1)
        d_w = jnp.where(causal, jnp.exp(c_all[:, col] + r_t[col, :]), 0.0)
        q = q_ref[0, :, h * dqk:(h + 1) * dqk]
        k = k_ref[0, :, h * dqk:(h + 1) * dqk]
        s = (_dot_nt(q, k) * (d_w * scale)).astype(BF16)
        k_w = (k.astype(F32) * w_s_all[:, col]).astype(BF16)
        return q, s, inter_w_all[:, col], floor_all[:, col], w_c_all[:, col], k_w

    def value_stage(h, q, s, inter_w, floor, w_c, k_w):
        v_ext = jnp.concatenate([v_ref[0, :, h * dv:(h + 1) * dv], ones_blk], axis=1)
        c_prev = c_ref[h]
        nd = _dot(s, v_ext) + inter_w * _dot(q, c_prev.astype(BF16))
        num, den = nd[:, :dv], nd[:, dv:dv + 1]
        hh = num / jnp.maximum(jnp.abs(den), floor)
        hh = _rmsnorm(hh, hn_ref[:, h * dv:(h + 1) * dv])
        gate_o = jax.nn.sigmoid(o_ref[0, :, h * dv:(h + 1) * dv].astype(F32))
        y_ref[0, :, h * dv:(h + 1) * dv] = (hh * gate_o).astype(BF16)
        c_ref[h] = w_c * c_prev + _dot_tn(k_w, v_ext)

    staged = None
    for h in range(heads + 1):
        nxt = gate_stage(h) if h < heads else None
        if staged is not None:
            value_stage(h - 1, *staged)
        staged = nxt


def _mlstm_core(p, gates, bias, hn, heads):
    bsz, s, _ = p.shape
    nq, nv = heads * MLSTM_QK, heads * MLSTM_V
    L = min(MLSTM_CHUNK, s)
    assert s % L == 0 and nv == 2 * nq
    return pl.pallas_call(
        functools.partial(_mlstm_body, heads=heads),
        grid=(bsz, s // L),
        in_specs=[
            pl.BlockSpec((1, L, nq), lambda b, c: (b, c, 0)),
            pl.BlockSpec((1, L, nq), lambda b, c: (b, c, 1)),
            pl.BlockSpec((1, L, nv), lambda b, c: (b, c, 1)),
            pl.BlockSpec((1, L, nv), lambda b, c: (b, c, 2)),
            pl.BlockSpec((1, L, 2 * LANES), lambda b, c: (b, c, 0)),
            pl.BlockSpec((1, 2 * LANES), lambda b, c: (0, 0)),
            pl.BlockSpec((1, nv), lambda b, c: (0, 0)),
        ],
        out_specs=pl.BlockSpec((1, L, nv), lambda b, c: (b, c, 0)),
        out_shape=jax.ShapeDtypeStruct((bsz, s, nv), BF16),
        scratch_shapes=[pltpu.VMEM((heads, MLSTM_QK, MLSTM_V + LANES), F32), pltpu.VMEM((1, LANES), F32)],
        compiler_params=_params("parallel", "arbitrary"),
        name="mlstm_core",
    )(p, p, p, p, gates, bias, hn)


FOX_BIAS_PIECES = 3


def _fox_gate_body(gate_ref, bias_ref, cp_ref, carry_ref, *, heads):
    @pl.when(pl.program_id(1) == 0)
    def _():
        carry_ref[...] = jnp.zeros_like(carry_ref)

    log_f = jax.nn.log_sigmoid(gate_ref[0] + bias_ref[...])
    cum = _cumsum_rows(log_f) + carry_ref[...]
    tc = cum.shape[0]
    carry_ref[...] = cum[tc - 1:tc, :]
    lane = lax.broadcasted_iota(jnp.int32, (tc, LANES), 1)
    inv_scale = -(FOX_HEAD ** 0.5)
    for h in range(heads):
        c = jnp.broadcast_to(cum[:, h:h + 1] * inv_scale, (tc, LANES))
        hi, mid, lo = (x.astype(F32) for x in _split3(c))
        pieces = jnp.where(lane == 0, hi, jnp.where(lane == 1, mid, jnp.where(lane == 2, lo, 0.0)))
        cp_ref[0, h] = pieces.astype(BF16)


def _fox_gates(gates, bias, heads):
    bsz, s, _ = gates.shape
    tc = min(FOX_GATE_CHUNK, s)
    assert s % tc == 0
    return pl.pallas_call(
        functools.partial(_fox_gate_body, heads=heads),
        grid=(bsz, s // tc),
        in_specs=[
            pl.BlockSpec((1, tc, LANES), lambda b, c: (b, c, 0)),
            pl.BlockSpec((1, LANES), lambda b, c: (0, 0)),
        ],
        out_specs=pl.BlockSpec((1, heads, tc, LANES), lambda b, c: (b, 0, c, 0)),
        out_shape=jax.ShapeDtypeStruct((bsz, heads, s, LANES), BF16),
        scratch_shapes=[pltpu.VMEM((1, LANES), F32)],
        compiler_params=_params("parallel", "arbitrary"),
        name="fox_gates",
    )(gates, bias)


def _fox_attn_body(q_ref, k_ref, v_ref, cp_ref, y_ref, sa_ref, sb_ref, m_ref, l_ref, acc_ref):
    tq = q_ref.shape[1]
    hps = q_ref.shape[2] // FOX_HEAD
    qi = pl.program_id(2)
    exp2_scale = (FOX_HEAD ** -0.5) * 1.4426950408889634
    lane = lax.broadcasted_iota(jnp.int32, (tq, LANES), 1)
    ones_cols = jnp.where(lane < FOX_BIAS_PIECES, 1.0, 0.0).astype(BF16)
    heads = [pl.ds(g * FOX_HEAD, FOX_HEAD) for g in range(hps)]
    q_ext = [jnp.concatenate([q_ref[0, :, cols], ones_cols], axis=1) for cols in heads]
    m_ref[...] = jnp.full_like(m_ref, NEG_BIG)
    l_ref[...] = jnp.zeros_like(l_ref)
    acc_ref[...] = jnp.zeros_like(acc_ref)

    def scores(j, dst_ref, masked):
        rows = pl.ds(pl.multiple_of(j * tq, tq), tq)
        for g, cols in enumerate(heads):
            k_ext = jnp.concatenate([k_ref[0, rows, cols], cp_ref[0, g, rows, :]], axis=1)
            s = _dot_nt(k_ext, q_ext[g])
            if masked:
                key = lax.broadcasted_iota(jnp.int32, (tq, tq), 0)
                qry = lax.broadcasted_iota(jnp.int32, (tq, tq), 1)
                s = jnp.where(key <= qry, s, NEG_BIG)
            dst_ref[g] = s

    def consume(j, src_ref):
        rows = pl.ds(pl.multiple_of(j * tq, tq), tq)
        for g, cols in enumerate(heads):
            m_old = m_ref[g]
            m_new = jnp.maximum(m_old, jnp.max(src_ref[g], axis=0, keepdims=True))
            alpha = jnp.exp2((m_old - m_new) * exp2_scale)
            p = jnp.exp2((src_ref[g] - m_new) * exp2_scale)
            l_ref[g] = alpha * l_ref[g] + jnp.sum(p, axis=0, keepdims=True)
            acc_ref[g] = alpha * acc_ref[g] + _dot_tn(v_ref[0, rows, cols], p.astype(BF16))
            m_ref[g] = m_new

    one = jnp.int32(1)
    pl.when(qi == 0)(lambda: scores(0, sa_ref, True))
    pl.when(qi > 0)(lambda: scores(0, sa_ref, False))
    n_pairs = lax.shift_right_logical(jnp.maximum(qi - 1, 0), one)

    def pair_body(i, carry):
        j = 2 * i
        scores(j + 1, sb_ref, False)
        consume(j, sa_ref)
        scores(j + 2, sa_ref, False)
        consume(j + 1, sb_ref)
        return carry

    lax.fori_loop(0, n_pairs, pair_body, 0)

    @pl.when(qi == 0)
    def _():
        consume(0, sa_ref)

    @pl.when(lax.bitwise_and(qi, one) == 1)
    def _():
        scores(qi, sb_ref, True)
        consume(qi - 1, sa_ref)
        consume(qi, sb_ref)

    @pl.when(jnp.logical_and(qi > 0, lax.bitwise_and(qi, one) == 0))
    def _():
        scores(qi - 1, sb_ref, False)
        consume(qi - 2, sa_ref)
        scores(qi, sa_ref, True)
        consume(qi - 1, sb_ref)
        consume(qi, sa_ref)

    for g, cols in enumerate(heads):
        y_ref[0, :, cols] = (acc_ref[g] / l_ref[g]).T.astype(BF16)


def _fox_attn(p, cp, heads):
    bsz, s, _ = p.shape
    tq = min(FOX_TQ, s)
    hps = FOX_HEADS_PER_STEP
    assert s % tq == 0 and heads % hps == 0
    hg, w = heads // hps, hps * FOX_HEAD
    return pl.pallas_call(
        _fox_attn_body,
        grid=(bsz, hg, s // tq),
        in_specs=[
            pl.BlockSpec((1, tq, w), lambda b, h, i: (b, i, h)),
            pl.BlockSpec((1, s, w), lambda b, h, i: (b, 0, hg + h)),
            pl.BlockSpec((1, s, w), lambda b, h, i: (b, 0, 2 * hg + h)),
            pl.BlockSpec((1, hps, s, LANES), lambda b, h, i: (b, h, 0, 0)),
        ],
        out_specs=pl.BlockSpec((1, tq, w), lambda b, h, i: (b, i, h)),
        out_shape=jax.ShapeDtypeStruct((bsz, s, heads * FOX_HEAD), BF16),
        scratch_shapes=[pltpu.VMEM((hps, tq, tq), F32), pltpu.VMEM((hps, tq, tq), F32),
                        pltpu.VMEM((hps, 1, tq), F32), pltpu.VMEM((hps, 1, tq), F32),
                        pltpu.VMEM((hps, FOX_HEAD, tq), F32)],
        compiler_params=_params("parallel", "parallel", "arbitrary"),
        name="fox_attn",
    )(p, p, p, cp)


def _gla_intra_t(q, k, b):
    L = q.shape[0]
    sub = min(GLA_SUB, L)

    a_t = jnp.zeros((L, L), F32)
    if L > sub:
        dk = q.shape[1]
        zeros = lambda n: [jnp.zeros((n, dk), BF16)] if n else []
        ks, qs = [], []
        for n in range(sub, L, sub):
            r = b[n - 1:n, :]
            k_part = (k[:n, :] * jnp.exp(r - b[:n, :])).astype(BF16)
            q_part = (q[n:n + sub, :] * jnp.exp(b[n:n + sub, :] - r)).astype(BF16)
            ks.append(jnp.concatenate([k_part] + zeros(L - n), axis=0))
            qs.append(jnp.concatenate(zeros(n) + [q_part] + zeros(L - n - sub), axis=0))
        a_t = _dot_nt(jnp.concatenate(ks, axis=1), jnp.concatenate(qs, axis=1))

    strip_t = lax.broadcasted_iota(jnp.int32, (sub, L), 1)
    strips = []
    for lo in range(0, L, sub):
        strip = jnp.zeros((sub, L), F32)
        for t in range(lo, lo + sub):
            hi = 8 * (t // 8 + 1)
            s_idx = lo + lax.broadcasted_iota(jnp.int32, (hi - lo, 1), 0)
            diff = jnp.where(s_idx <= t, b[t:t + 1, :] - b[lo:hi, :], NEG_BIG)
            w = jnp.exp(diff) * (k[lo:hi, :] * q[t:t + 1, :])
            colsum = jnp.sum(w, axis=1, keepdims=True)
            if hi < lo + sub:
                colsum = jnp.concatenate([colsum, jnp.zeros((lo + sub - hi, 1), F32)], axis=0)
            strip = jnp.where(strip_t == t, colsum, strip)
        strips.append(strip)
    return a_t + jnp.concatenate(strips, axis=0)


def _gla_body(q_ref, k_ref, v_ref, r_ref, alow_ref, wal_ref, bal_ref, hn_ref, y_ref, st_ref, *, heads):
    L = q_ref.shape[1]
    dk, dv = GLA_K, GLA_V
    qscale = dk ** -0.5

    @pl.when(pl.program_id(1) == 0)
    def _():
        st_ref[...] = jnp.zeros_like(st_ref)

    log_alpha = jax.nn.log_sigmoid(_dot_f32(alow_ref[0], wal_ref[...]) + bal_ref[...]) * (1.0 / GLA_GATE_TEMP)
    b_all = _cumsum_rows(log_alpha)

    def decay_stage(h):
        b = b_all[:, h * dk:(h + 1) * dk]
        q = q_ref[0, :, h * dk:(h + 1) * dk].astype(F32) * qscale
        k = k_ref[0, :, h * dk:(h + 1) * dk].astype(F32)
        a_t = _gla_intra_t(q, k, b).astype(BF16)
        b_last = b[L - 1:L, :]
        q_dec = (q * jnp.exp(b)).astype(BF16)
        k_dec = (k * jnp.exp(b_last - b)).astype(BF16)
        return a_t, q_dec, k_dec, jnp.exp(b_last)

    def value_stage(h, a_t, q_dec, k_dec, chunk_decay):
        v = v_ref[0, :, h * dv:(h + 1) * dv]
        st = st_ref[h]
        o = _dot_tn(a_t, v) + _dot_nt(q_dec, st.astype(BF16))
        st_ref[h] = chunk_decay * st + _dot_tn(v, k_dec)
        o = _rmsnorm(o, hn_ref[:, h * dv:(h + 1) * dv])
        r = r_ref[0, :, h * dv:(h + 1) * dv].astype(F32)
        y_ref[0, :, h * dv:(h + 1) * dv] = (o * (r * jax.nn.sigmoid(r))).astype(BF16)

    staged = None
    for h in range(heads + 1):
        nxt = decay_stage(h) if h < heads else None
        if staged is not None:
            value_stage(h - 1, *staged)
        staged = nxt


def _gla_core(p, alow, w_alpha, b_alpha, hn, heads):
    bsz, s, _ = p.shape
    nk, nv = heads * GLA_K, heads * GLA_V
    L = min(GLA_CHUNK, s)
    assert s % L == 0 and nv == 2 * nk
    return pl.pallas_call(
        functools.partial(_gla_body, heads=heads),
        grid=(bsz, s // L),
        in_specs=[
            pl.BlockSpec((1, L, nk), lambda b, c: (b, c, 0)),
            pl.BlockSpec((1, L, nk), lambda b, c: (b, c, 1)),
            pl.BlockSpec((1, L, nv), lambda b, c: (b, c, 1)),
            pl.BlockSpec((1, L, nv), lambda b, c: (b, c, 2)),
            pl.BlockSpec((1, L, LANES), lambda b, c: (b, c, 0)),
            pl.BlockSpec((LANES, nk), lambda b, c: (0, 0)),
            pl.BlockSpec((1, nk), lambda b, c: (0, 0)),
            pl.BlockSpec((1, nv), lambda b, c: (0, 0)),
        ],
        out_specs=pl.BlockSpec((1, L, nv), lambda b, c: (b, c, 0)),
        out_shape=jax.ShapeDtypeStruct((bsz, s, nv), BF16),
        scratch_shapes=[pltpu.VMEM((heads, GLA_V, GLA_K), F32)],
        compiler_params=_params("parallel", "arbitrary"),
        name="gla_core",
    )(p, p, p, p, alow, w_alpha, b_alpha, hn)


def _pad_cols(w, width, offsets):
    out = jnp.zeros((w.shape[0], width), w.dtype)
    c0 = 0
    for off, n in offsets:
        out = out.at[:, off:off + n].set(w[:, c0:c0 + n])
        c0 += n
    return out


def _mlstm_layer(h, bsz, norm_g, w_in, w_in_bf, j, b_if, head_norm, w_out_bf):
    t, d = h.shape
    heads = d // MLSTM_V
    n_main = 2 * heads * MLSTM_QK + 2 * heads * MLSTM_V
    wg = _pad_cols(w_in[:, n_main:], 2 * LANES, [(0, heads), (LANES, heads)])
    bias = _pad_cols(b_if.reshape(1, -1), 2 * LANES, [(0, heads), (LANES, heads)])
    p, gates = _proj(h, norm_g, w_in_bf, j, n_main, wg.astype(BF16))
    y = _mlstm_core(p.reshape(bsz, t // bsz, n_main), gates.reshape(bsz, t // bsz, 2 * LANES), bias,
                    head_norm.reshape(1, -1), heads)
    return _out_proj(h, y.reshape(t, -1), w_out_bf, j)


def _fox_layer(h, bsz, norm_g, w_in, w_in_bf, j, b_f, w_out_bf):
    t, d = h.shape
    heads = d // FOX_HEAD
    n_main = 3 * heads * FOX_HEAD
    wg = _pad_cols(w_in[:, n_main:], LANES, [(0, heads)])
    bias = _pad_cols(b_f.reshape(1, -1), LANES, [(0, heads)])
    p, gates = _proj(h, norm_g, w_in_bf, j, n_main, wg.astype(BF16))
    s = t // bsz
    cp = _fox_gates(gates.reshape(bsz, s, LANES), bias, heads)
    y = _fox_attn(p.reshape(bsz, s, n_main), cp, heads)
    return _out_proj(h, y.reshape(t, -1), w_out_bf, j)


def _gla_layer(h, bsz, norm_g, w_in, w_in_bf, j, w_alpha, b_alpha, head_norm, w_out_bf):
    t, d = h.shape
    heads = d // GLA_V
    n_main = 2 * heads * GLA_K + 2 * heads * GLA_V
    rank = w_alpha.shape[0]
    wg = _pad_cols(w_in[:, n_main:], LANES, [(0, rank)])
    wal = jnp.zeros((LANES, w_alpha.shape[1]), F32).at[:rank].set(w_alpha)
    p, alow = _proj(h, norm_g, w_in_bf, j, n_main, wg.astype(BF16))
    s = t // bsz
    y = _gla_core(p.reshape(bsz, s, n_main), alow.reshape(bsz, s, LANES), wal, b_alpha.reshape(1, -1),
                  head_norm.reshape(1, -1), heads)
    return _out_proj(h, y.reshape(t, -1), w_out_bf, j)


def kernel(x, ffn_norm, ffn_w_in, ffn_w_out, mix_norm, a_w_in, a_b_if, a_head_norm, a_w_out, b_w_in, b_b_f, b_w_out, c_w_in, c_w_alpha, c_b_alpha, c_head_norm, c_w_out, final_norm):
    bsz, s, d = x.shape
    depth = ffn_norm.shape[0]
    h = x.reshape(bsz * s, d)
    ffn_in_bf, ffn_out_bf = ffn_w_in.astype(BF16), ffn_w_out.astype(BF16)
    a_in_bf, a_out_bf = a_w_in.astype(BF16), a_w_out.astype(BF16)
    b_in_bf, b_out_bf = b_w_in.astype(BF16), b_w_out.astype(BF16)
    c_in_bf, c_out_bf = c_w_in.astype(BF16), c_w_out.astype(BF16)
    for i in range(depth):
        h = _ffn(h, ffn_norm[i, 0], ffn_in_bf, ffn_out_bf, (i, 0))
        kind, j = i % 3, i // 3
        if kind == 0:
            h = _mlstm_layer(h, bsz, mix_norm[i], a_w_in[j], a_in_bf, j, a_b_if[j], a_head_norm[j], a_out_bf)
        elif kind == 1:
            h = _fox_layer(h, bsz, mix_norm[i], b_w_in[j], b_in_bf, j, b_b_f[j], b_out_bf)
        else:
            h = _gla_layer(h, bsz, mix_norm[i], c_w_in[j], c_in_bf, j, c_w_alpha[j], c_b_alpha[j], c_head_norm[j],
                           c_out_bf)
        h = _ffn(h, ffn_norm[i, 1], ffn_in_bf, ffn_out_bf, (i, 1),
                 final_g=final_norm if i == depth - 1 else None)
    return h.reshape(bsz, s, d)
```

```python
import functools
import math

import jax
import jax.numpy as jnp
from jax import lax
from jax.experimental import pallas as pl
from jax.experimental.pallas import tpu as pltpu

F32 = jnp.float32
BF16 = jnp.bfloat16

RMS_EPS = 1e-6
FFN_RESIDUAL_WEIGHT = 0.5
GLA_GATE_TEMP = 16.0
NEG_BIG = -1e30

LANES = 128
MLSTM_QK, MLSTM_V = 128, 256
FOX_HEAD = 128
GLA_K, GLA_V = 256, 512

VMEM_LIMIT_BYTES = 60 * 1024 * 1024

FFN_TM, FFN_TF, FFN_ROWS = 1024, 512, 512
PROJ_TM, PROJ_TN, PROJ_ROWS = 1024, 2048, 512
OUT_TM = 512
MLSTM_CHUNK = 256
FOX_TQ = 512
FOX_HEADS_PER_STEP = 4
FOX_GATE_CHUNK = 512
GLA_CHUNK = 128
GLA_SUB = 16


def _params(*sem):
    return pltpu.CompilerParams(dimension_semantics=sem, vmem_limit_bytes=VMEM_LIMIT_BYTES)


def _tile(n, pref):
    if n <= pref:
        return n
    return max(c for c in range(LANES, pref + 1, LANES) if n % c == 0)


def _dot(a, b):
    return jnp.dot(a, b, preferred_element_type=F32)


def _dot_nt(a, b):
    return lax.dot_general(a, b, (((1,), (1,)), ((), ())), preferred_element_type=F32)


def _dot_tn(a, b):
    return lax.dot_general(a, b, (((0,), (0,)), ((), ())), preferred_element_type=F32)


def _split2(x):
    hi = x.astype(BF16)
    lo = (x - hi.astype(F32)).astype(BF16)
    return hi, lo


def _split3(x):
    hi = x.astype(BF16)
    r = x - hi.astype(F32)
    mid = r.astype(BF16)
    lo = (r - mid.astype(F32)).astype(BF16)
    return hi, mid, lo


def _dot_f32(a, b):
    ah, al = _split2(a)
    bh, bl = _split2(b)
    return _dot(ah, bh) + (_dot(ah, bl) + _dot(al, bh))


def _cumsum_rows(x):
    n = x.shape[0]
    tril = (lax.broadcasted_iota(jnp.int32, (n, n), 0) >= lax.broadcasted_iota(jnp.int32, (n, n), 1)).astype(BF16)
    hi, mid, lo = _split3(x)
    return _dot(tril, hi) + (_dot(tril, mid) + _dot(tril, lo))


def _cumsum_lanes(x):
    n = x.shape[1]
    triu = (lax.broadcasted_iota(jnp.int32, (n, n), 0) <= lax.broadcasted_iota(jnp.int32, (n, n), 1)).astype(BF16)
    hi, mid, lo = _split3(x)
    return _dot(hi, triu) + (_dot(mid, triu) + _dot(lo, triu))


def _rmsnorm(x, g):
    return x * lax.rsqrt(jnp.mean(x * x, axis=-1, keepdims=True) + RMS_EPS) * g


def _ffn_body(x_ref, g_ref, wa_ref, wb_ref, wo_ref, fg_ref, o_ref, u_ref, *, nf, final_norm):
    f = pl.program_id(1)
    tm = x_ref.shape[0]
    rs = min(FFN_ROWS, tm)
    row_tiles = [pl.ds(r * rs, rs) for r in range(tm // rs)]

    def step(first, last):
        for rows in row_tiles:
            if first:
                u = _rmsnorm(x_ref[rows, :], g_ref[...]).astype(BF16)
                u_ref[rows, :] = u
            else:
                u = u_ref[rows, :]
            a = _dot(u, wa_ref[...])
            b = _dot(u, wb_ref[...])
            hid = (a * jax.nn.sigmoid(a) * b).astype(BF16)
            acc = _dot(hid, wo_ref[...])
            if not first:
                acc = o_ref[rows, :] + acc
            if last:
                acc = x_ref[rows, :] + FFN_RESIDUAL_WEIGHT * acc
                if final_norm:
                    acc = _rmsnorm(acc, fg_ref[...])
            o_ref[rows, :] = acc

    if nf == 1:
        step(True, True)
    else:
        pl.when(f == 0)(lambda: step(True, False))
        if nf > 2:
            pl.when(jnp.logical_and(f > 0, f < nf - 1))(lambda: step(False, False))
        pl.when(f == nf - 1)(lambda: step(False, True))


def _ffn(h, g, w_in, w_out, layer, final_g=None):
    t, d = h.shape
    dff = w_out.shape[-2]
    tm, tf = _tile(t, FFN_TM), _tile(dff, FFN_TF)
    assert t % tm == 0 and dff % tf == 0
    nf = dff // tf
    fg = g if final_g is None else final_g
    li, lj = layer
    return pl.pallas_call(
        functools.partial(_ffn_body, nf=nf, final_norm=final_g is not None),
        grid=(t // tm, nf),
        in_specs=[
            pl.BlockSpec((tm, d), lambda i, f: (i, 0)),
            pl.BlockSpec((1, d), lambda i, f: (0, 0)),
            pl.BlockSpec((None, None, d, tf), lambda i, f: (li, lj, 0, f)),
            pl.BlockSpec((None, None, d, tf), lambda i, f: (li, lj, 0, f + nf)),
            pl.BlockSpec((None, None, tf, d), lambda i, f: (li, lj, f, 0)),
            pl.BlockSpec((1, d), lambda i, f: (0, 0)),
        ],
        out_specs=pl.BlockSpec((tm, d), lambda i, f: (i, 0)),
        out_shape=jax.ShapeDtypeStruct((t, d), F32),
        scratch_shapes=[pltpu.VMEM((tm, d), BF16)],
        compiler_params=_params("parallel", "arbitrary"),
        name="ffn",
    )(h, g.reshape(1, d), w_in, w_in, w_out, fg.reshape(1, d))


def _proj_body(x_ref, g_ref, w_ref, wg_ref, p_ref, gate_ref, u_ref):
    tm = x_ref.shape[0]
    rs = min(PROJ_ROWS, tm)

    def step(first):
        for r in range(tm // rs):
            rows = pl.ds(r * rs, rs)
            if first:
                u = _rmsnorm(x_ref[rows, :], g_ref[...]).astype(BF16)
                u_ref[rows, :] = u
                gate_ref[rows, :] = _dot(u, wg_ref[...])
            else:
                u = u_ref[rows, :]
            p_ref[rows, :] = _dot(u, w_ref[...]).astype(BF16)

    pl.when(pl.program_id(1) == 0)(lambda: step(True))
    pl.when(pl.program_id(1) > 0)(lambda: step(False))


def _proj(h, g, w, layer, n, wg):
    t, d = h.shape
    gw = wg.shape[1]
    tm, tn = _tile(t, PROJ_TM), _tile(n, PROJ_TN)
    assert t % tm == 0 and n % tn == 0
    return pl.pallas_call(
        _proj_body,
        grid=(t // tm, n // tn),
        in_specs=[
            pl.BlockSpec((tm, d), lambda i, j: (i, 0)),
            pl.BlockSpec((1, d), lambda i, j: (0, 0)),
            pl.BlockSpec((None, d, tn), lambda i, j: (layer, 0, j)),
            pl.BlockSpec((d, gw), lambda i, j: (0, 0)),
        ],
        out_specs=[
            pl.BlockSpec((tm, tn), lambda i, j: (i, j)),
            pl.BlockSpec((tm, gw), lambda i, j: (i, 0)),
        ],
        out_shape=[jax.ShapeDtypeStruct((t, n), BF16), jax.ShapeDtypeStruct((t, gw), F32)],
        scratch_shapes=[pltpu.VMEM((tm, d), BF16)],
        compiler_params=_params("parallel", "arbitrary"),
        name="mixer_in_proj",
    )(h, g.reshape(1, d), w, wg)


def _out_body(h_ref, y_ref, w_ref, o_ref):
    o_ref[...] = h_ref[...] + _dot(y_ref[...], w_ref[...])


def _out_proj(h, y, w, layer):
    t, d = h.shape
    k = y.shape[1]
    tm = min(OUT_TM, t)
    assert t % tm == 0
    return pl.pallas_call(
        _out_body,
        grid=(t // tm,),
        in_specs=[
            pl.BlockSpec((tm, d), lambda i: (i, 0)),
            pl.BlockSpec((tm, k), lambda i: (i, 0)),
            pl.BlockSpec((None, k, d), lambda i: (layer, 0, 0)),
        ],
        out_specs=pl.BlockSpec((tm, d), lambda i: (i, 0)),
        out_shape=jax.ShapeDtypeStruct((t, d), F32),
        compiler_params=_params("parallel"),
        name="mixer_out_proj",
    )(h, y, w)


def _mlstm_body(q_ref, k_ref, v_ref, o_ref, gate_ref, bias_ref, hn_ref, y_ref, c_ref, m_ref, *, heads):
    L = q_ref.shape[1]
    dqk, dv = MLSTM_QK, MLSTM_V
    scale = dqk ** -0.5

    @pl.when(pl.program_id(1) == 0)
    def _():
        c_ref[...] = jnp.zeros_like(c_ref)
        m_ref[...] = jnp.zeros_like(m_ref)

    gates = gate_ref[0] + bias_ref[...]
    i_pre = gates[:, :LANES]
    b_cum = _cumsum_rows(jax.nn.log_sigmoid(gates[:, LANES:]))
    m_all = m_ref[...]
    row = lax.broadcasted_iota(jnp.int32, (L, L), 0)
    col = lax.broadcasted_iota(jnp.int32, (L, L), 1)
    causal = col <= row
    ones_blk = jnp.ones((L, LANES), BF16)

    inter_log_all = b_cum + m_all
    b_last = b_cum[L - 1:L, :]
    g = b_last - b_cum + i_pre
    m_new = jnp.maximum(b_last + m_all, jnp.max(g, axis=0, keepdims=True))
    w_c_all = jnp.exp(b_last + m_all - m_new)
    w_s_all = jnp.exp(g - m_new) * scale
    m_ref[...] = m_new

    r_all = i_pre - b_cum
    r_t = r_all.T
    time = lax.broadcasted_iota(jnp.int32, (L, LANES), 0)
    run_max, step = r_all, 1
    while step < L:
        run_max = jnp.where(time >= step, jnp.maximum(run_max, pltpu.roll(run_max, step, 0)), run_max)
        step *= 2
    m_t_all = jnp.maximum(inter_log_all, b_cum + run_max)
    c_all = b_cum - m_t_all + math.log(scale)
    inter_w_all = jnp.exp(inter_log_all - m_t_all)
    floor_all = jnp.exp(-m_t_all)

    def gate_stage(h):
        col = slice(h, h + 1)
        d_w = jnp.where(causal, jnp.exp(c_all[:, col] + r_t[col, :]), 0.0)
        q = q_ref[0, :, h * dqk:(h + 1) * dqk]
        k = k_ref[0, :, h * dqk:(h + 1) * dqk]
        s = (_dot_nt(q, k) * d_w).astype(BF16)
        k_w = (k.astype(F32) * w_s_all[:, col]).astype(BF16)
        return q, s, inter_w_all[:, col], floor_all[:, col], w_c_all[:, col], k_w

    def value_stage(h, q, s, inter_w, floor, w_c, k_w):
        v_ext = jnp.concatenate([v_ref[0, :, h * dv:(h + 1) * dv], ones_blk], axis=1)
        c_prev = c_ref[h]
        nd = _dot(s, v_ext) + inter_w * _dot(q, c_prev.astype(BF16))
        num, den = nd[:, :dv], nd[:, dv:dv + 1]
        hh = num / jnp.maximum(jnp.abs(den), floor)
        hh = _rmsnorm(hh, hn_ref[:, h * dv:(h + 1) * dv])
        gate_o = jax.nn.sigmoid(o_ref[0, :, h * dv:(h + 1) * dv].astype(F32))
        y_ref[0, :, h * dv:(h + 1) * dv] = (hh * gate_o).astype(BF16)
        c_ref[h] = w_c * c_prev + _dot_tn(k_w, v_ext)

    staged = None
    for h in range(heads + 1):
        nxt = gate_stage(h) if h < heads else None
        if staged is not None:
            value_stage(h - 1, *staged)
        staged = nxt


def _mlstm_core(p, gates, bias, hn, heads):
    bsz, s, _ = p.shape
    nq, nv = heads * MLSTM_QK, heads * MLSTM_V
    L = min(MLSTM_CHUNK, s)
    assert s % L == 0 and nv == 2 * nq
    return pl.pallas_call(
        functools.partial(_mlstm_body, heads=heads),
        grid=(bsz, s // L),
        in_specs=[
            pl.BlockSpec((1, L, nq), lambda b, c: (b, c, 0)),
            pl.BlockSpec((1, L, nq), lambda b, c: (b, c, 1)),
            pl.BlockSpec((1, L, nv), lambda b, c: (b, c, 1)),
            pl.BlockSpec((1, L, nv), lambda b, c: (b, c, 2)),
            pl.BlockSpec((1, L, 2 * LANES), lambda b, c: (b, c, 0)),
            pl.BlockSpec((1, 2 * LANES), lambda b, c: (0, 0)),
            pl.BlockSpec((1, nv), lambda b, c: (0, 0)),
        ],
        out_specs=pl.BlockSpec((1, L, nv), lambda b, c: (b, c, 0)),
        out_shape=jax.ShapeDtypeStruct((bsz, s, nv), BF16),
        scratch_shapes=[pltpu.VMEM((heads, MLSTM_QK, MLSTM_V + LANES), F32), pltpu.VMEM((1, LANES), F32)],
        compiler_params=_params("parallel", "arbitrary"),
        name="mlstm_core",
    )(p, p, p, p, gates, bias, hn)


FOX_BIAS_PIECES = 3


def _fox_gate_body(gate_ref, bias_ref, cp_ref, carry_ref, *, heads):
    @pl.when(pl.program_id(1) == 0)
    def _():
        carry_ref[...] = jnp.zeros_like(carry_ref)

    log_f = jax.nn.log_sigmoid(gate_ref[0] + bias_ref[...])
    cum = _cumsum_rows(log_f) + carry_ref[...]
    tc = cum.shape[0]
    carry_ref[...] = cum[tc - 1:tc, :]
    lane = lax.broadcasted_iota(jnp.int32, (tc, LANES), 1)
    neg_log2e = -1.4426950408889634
    for h in range(heads):
        c = jnp.broadcast_to(cum[:, h:h + 1] * neg_log2e, (tc, LANES))
        hi, mid, lo = (x.astype(F32) for x in _split3(c))
        pieces = jnp.where(lane == 0, hi, jnp.where(lane == 1, mid, jnp.where(lane == 2, lo, 0.0)))
        cp_ref[0, h] = pieces.astype(BF16)


def _fox_gates(gates, bias, heads):
    bsz, s, _ = gates.shape
    tc = min(FOX_GATE_CHUNK, s)
    assert s % tc == 0
    return pl.pallas_call(
        functools.partial(_fox_gate_body, heads=heads),
        grid=(bsz, s // tc),
        in_specs=[
            pl.BlockSpec((1, tc, LANES), lambda b, c: (b, c, 0)),
            pl.BlockSpec((1, LANES), lambda b, c: (0, 0)),
        ],
        out_specs=pl.BlockSpec((1, heads, tc, LANES), lambda b, c: (b, 0, c, 0)),
        out_shape=jax.ShapeDtypeStruct((bsz, heads, s, LANES), BF16),
        scratch_shapes=[pltpu.VMEM((1, LANES), F32)],
        compiler_params=_params("parallel", "arbitrary"),
        name="fox_gates",
    )(gates, bias)


def _fox_attn_body(q_ref, k_ref, v_ref, cp_ref, y_ref, sa_ref, sb_ref, m_ref, l_ref, acc_ref):
    tq = q_ref.shape[1]
    hps = q_ref.shape[2] // FOX_HEAD
    qi = pl.program_id(2)
    q_scale = (FOX_HEAD ** -0.5) * 1.4426950408889634
    lane = lax.broadcasted_iota(jnp.int32, (tq, LANES), 1)
    ones_cols = jnp.where(lane < FOX_BIAS_PIECES, 1.0, 0.0).astype(BF16)
    heads = [pl.ds(g * FOX_HEAD, FOX_HEAD) for g in range(hps)]
    q_ext = [jnp.concatenate([(q_ref[0, :, cols].astype(F32) * q_scale).astype(BF16), ones_cols], axis=1)
             for cols in heads]
    m_ref[...] = jnp.full_like(m_ref, NEG_BIG)
    l_ref[...] = jnp.zeros_like(l_ref)
    acc_ref[...] = jnp.zeros_like(acc_ref)

    def scores(j, dst_ref, masked):
        rows = pl.ds(pl.multiple_of(j * tq, tq), tq)
        for g, cols in enumerate(heads):
            k_ext = jnp.concatenate([k_ref[0, rows, cols], cp_ref[0, g, rows, :]], axis=1)
            s = _dot_nt(k_ext, q_ext[g])
            if masked:
                key = lax.broadcasted_iota(jnp.int32, (tq, tq), 0)
                qry = lax.broadcasted_iota(jnp.int32, (tq, tq), 1)
                s = jnp.where(key <= qry, s, NEG_BIG)
            dst_ref[g] = s

    def consume(j, src_ref):
        rows = pl.ds(pl.multiple_of(j * tq, tq), tq)
        for g, cols in enumerate(heads):
            m_old = m_ref[g]
            m_new = jnp.maximum(m_old, jnp.max(src_ref[g], axis=0, keepdims=True))
            alpha = jnp.exp2(m_old - m_new)
            p = jnp.exp2(src_ref[g] - m_new)
            l_ref[g] = alpha * l_ref[g] + jnp.sum(p, axis=0, keepdims=True)
            acc_ref[g] = alpha * acc_ref[g] + _dot_tn(v_ref[0, rows, cols], p.astype(BF16))
            m_ref[g] = m_new

    one = jnp.int32(1)
    pl.when(qi == 0)(lambda: scores(0, sa_ref, True))
    pl.when(qi > 0)(lambda: scores(0, sa_ref, False))
    n_pairs = lax.shift_right_logical(jnp.maximum(qi - 1, 0), one)

    def pair_body(i, carry):
        j = 2 * i
        scores(j + 1, sb_ref, False)
        consume(j, sa_ref)
        scores(j + 2, sa_ref, False)
        consume(j + 1, sb_ref)
        return carry

    lax.fori_loop(0, n_pairs, pair_body, 0)

    @pl.when(qi == 0)
    def _():
        consume(0, sa_ref)

    @pl.when(lax.bitwise_and(qi, one) == 1)
    def _():
        scores(qi, sb_ref, True)
        consume(qi - 1, sa_ref)
        consume(qi, sb_ref)

    @pl.when(jnp.logical_and(qi > 0, lax.bitwise_and(qi, one) == 0))
    def _():
        scores(qi - 1, sb_ref, False)
        consume(qi - 2, sa_ref)
        scores(qi, sa_ref, True)
        consume(qi - 1, sb_ref)
        consume(qi, sa_ref)

    for g, cols in enumerate(heads):
        y_ref[0, :, cols] = (acc_ref[g] / l_ref[g]).T.astype(BF16)


def _fox_attn(p, cp, heads):
    bsz, s, _ = p.shape
    tq = min(FOX_TQ, s)
    hps = FOX_HEADS_PER_STEP
    assert s % tq == 0 and heads % hps == 0
    hg, w = heads // hps, hps * FOX_HEAD
    return pl.pallas_call(
        _fox_attn_body,
        grid=(bsz, hg, s // tq),
        in_specs=[
            pl.BlockSpec((1, tq, w), lambda b, h, i: (b, i, h)),
            pl.BlockSpec((1, s, w), lambda b, h, i: (b, 0, hg + h)),
            pl.BlockSpec((1, s, w), lambda b, h, i: (b, 0, 2 * hg + h)),
            pl.BlockSpec((1, hps, s, LANES), lambda b, h, i: (b, h, 0, 0)),
        ],
        out_specs=pl.BlockSpec((1, tq, w), lambda b, h, i: (b, i, h)),
        out_shape=jax.ShapeDtypeStruct((bsz, s, heads * FOX_HEAD), BF16),
        scratch_shapes=[pltpu.VMEM((hps, tq, tq), F32), pltpu.VMEM((hps, tq, tq), F32),
                        pltpu.VMEM((hps, 1, tq), F32), pltpu.VMEM((hps, 1, tq), F32),
                        pltpu.VMEM((hps, FOX_HEAD, tq), F32)],
        compiler_params=_params("parallel", "parallel", "arbitrary"),
        name="fox_attn",
    )(p, p, p, cp)


def _gla_intra_t(q, k, b):
    L = q.shape[0]
    sub = min(GLA_SUB, L)

    a_t = jnp.zeros((L, L), F32)
    if L > sub:
        dk = q.shape[1]
        zeros = lambda n: [jnp.zeros((n, dk), BF16)] if n else []
        ks, qs = [], []
        for n in range(sub, L, sub):
            r = b[n - 1:n, :]
            k_part = (k[:n, :] * jnp.exp(r - b[:n, :])).astype(BF16)
            q_part = (q[n:n + sub, :] * jnp.exp(b[n:n + sub, :] - r)).astype(BF16)
            ks.append(jnp.concatenate([k_part] + zeros(L - n), axis=0))
            qs.append(jnp.concatenate(zeros(n) + [q_part] + zeros(L - n - sub), axis=0))
        a_t = _dot_nt(jnp.concatenate(ks, axis=1), jnp.concatenate(qs, axis=1))

    strip_t = lax.broadcasted_iota(jnp.int32, (sub, L), 1)
    strips = []
    for lo in range(0, L, sub):
        strip = jnp.zeros((sub, L), F32)
        for t in range(lo, lo + sub):
            hi = 8 * (t // 8 + 1)
            s_idx = lo + lax.broadcasted_iota(jnp.int32, (hi - lo, 1), 0)
            diff = jnp.where(s_idx <= t, b[t:t + 1, :] - b[lo:hi, :], NEG_BIG)
            w = jnp.exp(diff) * (k[lo:hi, :] * q[t:t + 1, :])
            colsum = jnp.sum(w, axis=1, keepdims=True)
            if hi < lo + sub:
                colsum = jnp.concatenate([colsum, jnp.zeros((lo + sub - hi, 1), F32)], axis=0)
            strip = jnp.where(strip_t == t, colsum, strip)
        strips.append(strip)
    return a_t + jnp.concatenate(strips, axis=0)


def _gla_body(q_ref, k_ref, v_ref, r_ref, alow_ref, wal_ref, bal_ref, hn_ref, y_ref, st_ref, *, heads):
    L = q_ref.shape[1]
    dk, dv = GLA_K, GLA_V
    qscale = dk ** -0.5

    @pl.when(pl.program_id(1) == 0)
    def _():
        st_ref[...] = jnp.zeros_like(st_ref)

    log_alpha = jax.nn.log_sigmoid(_dot_f32(alow_ref[0], wal_ref[...]) + bal_ref[...]) * (1.0 / GLA_GATE_TEMP)
    b_all = _cumsum_rows(log_alpha)

    def decay_stage(h):
        b = b_all[:, h * dk:(h + 1) * dk]
        q = q_ref[0, :, h * dk:(h + 1) * dk].astype(F32) * qscale
        k = k_ref[0, :, h * dk:(h + 1) * dk].astype(F32)
        a_t = _gla_intra_t(q, k, b).astype(BF16)
        b_last = b[L - 1:L, :]
        q_dec = (q * jnp.exp(b)).astype(BF16)
        k_dec = (k * jnp.exp(b_last - b)).astype(BF16)
        return a_t, q_dec, k_dec, jnp.exp(b_last)

    def value_stage(h, a_t, q_dec, k_dec, chunk_decay):
        v = v_ref[0, :, h * dv:(h + 1) * dv]
        st = st_ref[h]
        o = _dot_tn(a_t, v) + _dot_nt(q_dec, st.astype(BF16))
        st_ref[h] = chunk_decay * st + _dot_tn(v, k_dec)
        o = _rmsnorm(o, hn_ref[:, h * dv:(h + 1) * dv])
        r = r_ref[0, :, h * dv:(h + 1) * dv].astype(F32)
        y_ref[0, :, h * dv:(h + 1) * dv] = (o * (r * jax.nn.sigmoid(r))).astype(BF16)

    staged = None
    for h in range(heads + 1):
        nxt = decay_stage(h) if h < heads else None
        if staged is not None:
            value_stage(h - 1, *staged)
        staged = nxt


def _gla_core(p, alow, w_alpha, b_alpha, hn, heads):
    bsz, s, _ = p.shape
    nk, nv = heads * GLA_K, heads * GLA_V
    L = min(GLA_CHUNK, s)
    assert s % L == 0 and nv == 2 * nk
    return pl.pallas_call(
        functools.partial(_gla_body, heads=heads),
        grid=(bsz, s // L),
        in_specs=[
            pl.BlockSpec((1, L, nk), lambda b, c: (b, c, 0)),
            pl.BlockSpec((1, L, nk), lambda b, c: (b, c, 1)),
            pl.BlockSpec((1, L, nv), lambda b, c: (b, c, 1)),
            pl.BlockSpec((1, L, nv), lambda b, c: (b, c, 2)),
            pl.BlockSpec((1, L, LANES), lambda b, c: (b, c, 0)),
            pl.BlockSpec((LANES, nk), lambda b, c: (0, 0)),
            pl.BlockSpec((1, nk), lambda b, c: (0, 0)),
            pl.BlockSpec((1, nv), lambda b, c: (0, 0)),
        ],
        out_specs=pl.BlockSpec((1, L, nv), lambda b, c: (b, c, 0)),
        out_shape=jax.ShapeDtypeStruct((bsz, s, nv), BF16),
        scratch_shapes=[pltpu.VMEM((heads, GLA_V, GLA_K), F32)],
        compiler_params=_params("parallel", "arbitrary"),
        name="gla_core",
    )(p, p, p, p, alow, w_alpha, b_alpha, hn)


def _pad_cols(w, width, offsets):
    out = jnp.zeros((w.shape[0], width), w.dtype)
    c0 = 0
    for off, n in offsets:
        out = out.at[:, off:off + n].set(w[:, c0:c0 + n])
        c0 += n
    return out


def _mlstm_layer(h, bsz, norm_g, w_in, w_in_bf, j, b_if, head_norm, w_out_bf):
    t, d = h.shape
    heads = d // MLSTM_V
    n_main = 2 * heads * MLSTM_QK + 2 * heads * MLSTM_V
    wg = _pad_cols(w_in[:, n_main:], 2 * LANES, [(0, heads), (LANES, heads)])
    bias = _pad_cols(b_if.reshape(1, -1), 2 * LANES, [(0, heads), (LANES, heads)])
    p, gates = _proj(h, norm_g, w_in_bf, j, n_main, wg.astype(BF16))
    y = _mlstm_core(p.reshape(bsz, t // bsz, n_main), gates.reshape(bsz, t // bsz, 2 * LANES), bias,
                    head_norm.reshape(1, -1), heads)
    return _out_proj(h, y.reshape(t, -1), w_out_bf, j)


def _fox_layer(h, bsz, norm_g, w_in, w_in_bf, j, b_f, w_out_bf):
    t, d = h.shape
    heads = d // FOX_HEAD
    n_main = 3 * heads * FOX_HEAD
    wg = _pad_cols(w_in[:, n_main:], LANES, [(0, heads)])
    bias = _pad_cols(b_f.reshape(1, -1), LANES, [(0, heads)])
    p, gates = _proj(h, norm_g, w_in_bf, j, n_main, wg.astype(BF16))
    s = t // bsz
    cp = _fox_gates(gates.reshape(bsz, s, LANES), bias, heads)
    y = _fox_attn(p.reshape(bsz, s, n_main), cp, heads)
    return _out_proj(h, y.reshape(t, -1), w_out_bf, j)


def _gla_layer(h, bsz, norm_g, w_in, w_in_bf, j, w_alpha, b_alpha, head_norm, w_out_bf):
    t, d = h.shape
    heads = d // GLA_V
    n_main = 2 * heads * GLA_K + 2 * heads * GLA_V
    rank = w_alpha.shape[0]
    wg = _pad_cols(w_in[:, n_main:], LANES, [(0, rank)])
    wal = jnp.zeros((LANES, w_alpha.shape[1]), F32).at[:rank].set(w_alpha)
    p, alow = _proj(h, norm_g, w_in_bf, j, n_main, wg.astype(BF16))
    s = t // bsz
    y = _gla_core(p.reshape(bsz, s, n_main), alow.reshape(bsz, s, LANES), wal, b_alpha.reshape(1, -1),
                  head_norm.reshape(1, -1), heads)
    return _out_proj(h, y.reshape(t, -1), w_out_bf, j)


def kernel(x, ffn_norm, ffn_w_in, ffn_w_out, mix_norm, a_w_in, a_b_if, a_head_norm, a_w_out, b_w_in, b_b_f, b_w_out, c_w_in, c_w_alpha, c_b_alpha, c_head_norm, c_w_out, final_norm):
    bsz, s, d = x.shape
    depth = ffn_norm.shape[0]
    h = x.reshape(bsz * s, d)
    ffn_in_bf, ffn_out_bf = ffn_w_in.astype(BF16), ffn_w_out.astype(BF16)
    a_in_bf, a_out_bf = a_w_in.astype(BF16), a_w_out.astype(BF16)
    b_in_bf, b_out_bf = b_w_in.astype(BF16), b_w_out.astype(BF16)
    c_in_bf, c_out_bf = c_w_in.astype(BF16), c_w_out.astype(BF16)
    for i in range(depth):
        h = _ffn(h, ffn_norm[i, 0], ffn_in_bf, ffn_out_bf, (i, 0))
        kind, j = i % 3, i // 3
        if kind == 0:
            h = _mlstm_layer(h, bsz, mix_norm[i], a_w_in[j], a_in_bf, j, a_b_if[j], a_head_norm[j], a_out_bf)
        elif kind == 1:
            h = _fox_layer(h, bsz, mix_norm[i], b_w_in[j], b_in_bf, j, b_b_f[j], b_out_bf)
        else:
            h = _gla_layer(h, bsz, mix_norm[i], c_w_in[j], c_in_bf, j, c_w_alpha[j], c_b_alpha[j], c_head_norm[j],
                           c_out_bf)
        h = _ffn(h, ffn_norm[i, 1], ffn_in_bf, ffn_out_bf, (i, 1),
                 final_g=final_norm if i == depth - 1 else None)
    return h.reshape(bsz, s, d)
```

```python
import functools
import math

import jax
import jax.numpy as jnp
from jax import lax
from jax.experimental import pallas as pl
from jax.experimental.pallas import tpu as pltpu

F32 = jnp.float32
BF16 = jnp.bfloat16

RMS_EPS = 1e-6
FFN_RESIDUAL_WEIGHT = 0.5
GLA_GATE_TEMP = 16.0
NEG_BIG = -1e30

LANES = 128
MLSTM_QK, MLSTM_V = 128, 256
FOX_HEAD = 128
GLA_K, GLA_V = 256, 512

VMEM_LIMIT_BYTES = 62 * 1024 * 1024

FFN_TM, FFN_TF, FFN_ROWS = 1024, 512, 512
PROJ_TM, PROJ_TN, PROJ_ROWS = 1024, 2048, 512
OUT_TM = 512
MLSTM_CHUNK = 256
FOX_TQ = 512
FOX_HEADS_PER_STEP = 4
FOX_GATE_CHUNK = 512
GLA_CHUNK = 128
GLA_SUB = 16


def _params(*sem):
    return pltpu.CompilerParams(dimension_semantics=sem, vmem_limit_bytes=VMEM_LIMIT_BYTES)


def _tile(n, pref):
    if n <= pref:
        return n
    return max(c for c in range(LANES, pref + 1, LANES) if n % c == 0)


def _dot(a, b):
    return jnp.dot(a, b, preferred_element_type=F32)


def _dot_nt(a, b):
    return lax.dot_general(a, b, (((1,), (1,)), ((), ())), preferred_element_type=F32)


def _dot_tn(a, b):
    return lax.dot_general(a, b, (((0,), (0,)), ((), ())), preferred_element_type=F32)


def _split2(x):
    hi = x.astype(BF16)
    lo = (x - hi.astype(F32)).astype(BF16)
    return hi, lo


def _split3(x):
    hi = x.astype(BF16)
    r = x - hi.astype(F32)
    mid = r.astype(BF16)
    lo = (r - mid.astype(F32)).astype(BF16)
    return hi, mid, lo


def _dot_f32(a, b):
    ah, al = _split2(a)
    bh, bl = _split2(b)
    return _dot(ah, bh) + (_dot(ah, bl) + _dot(al, bh))


def _cumsum_rows(x):
    n = x.shape[0]
    tril = (lax.broadcasted_iota(jnp.int32, (n, n), 0) >= lax.broadcasted_iota(jnp.int32, (n, n), 1)).astype(BF16)
    hi, mid, lo = _split3(x)
    return _dot(tril, hi) + (_dot(tril, mid) + _dot(tril, lo))


def _rmsnorm(x, g):
    return x * lax.rsqrt(jnp.mean(x * x, axis=-1, keepdims=True) + RMS_EPS) * g


def _ffn_body(x_ref, g_ref, wa_ref, wb_ref, wo_ref, fg_ref, o_ref, u_ref, *, nf, final_norm):
    f = pl.program_id(1)
    tm = x_ref.shape[0]
    rs = min(FFN_ROWS, tm)
    row_tiles = [pl.ds(r * rs, rs) for r in range(tm // rs)]

    def step(first, last):
        for rows in row_tiles:
            if first:
                u = _rmsnorm(x_ref[rows, :], g_ref[...]).astype(BF16)
                u_ref[rows, :] = u
            else:
                u = u_ref[rows, :]
            a = _dot(u, wa_ref[...].astype(BF16))
            b = _dot(u, wb_ref[...].astype(BF16))
            hid = (a * jax.nn.sigmoid(a) * b).astype(BF16)
            acc = _dot(hid, wo_ref[...])
            if not first:
                acc = o_ref[rows, :] + acc
            if last:
                acc = x_ref[rows, :] + FFN_RESIDUAL_WEIGHT * acc
                if final_norm:
                    acc = _rmsnorm(acc, fg_ref[...])
            o_ref[rows, :] = acc

    if nf == 1:
        step(True, True)
    else:
        pl.when(f == 0)(lambda: step(True, False))
        if nf > 2:
            pl.when(jnp.logical_and(f > 0, f < nf - 1))(lambda: step(False, False))
        pl.when(f == nf - 1)(lambda: step(False, True))


def _ffn(h, g, w_in, w_out, layer, final_g=None):
    t, d = h.shape
    dff = w_out.shape[-2]
    tm, tf = _tile(t, FFN_TM), _tile(dff, FFN_TF)
    assert t % tm == 0 and dff % tf == 0
    nf = dff // tf
    fg = g if final_g is None else final_g
    li, lj = layer
    return pl.pallas_call(
        functools.partial(_ffn_body, nf=nf, final_norm=final_g is not None),
        grid=(t // tm, nf),
        in_specs=[
            pl.BlockSpec((tm, d), lambda i, f: (i, 0)),
            pl.BlockSpec((1, d), lambda i, f: (0, 0)),
            pl.BlockSpec((None, None, d, tf), lambda i, f: (li, lj, 0, f)),
            pl.BlockSpec((None, None, d, tf), lambda i, f: (li, lj, 0, f + nf)),
            pl.BlockSpec((None, None, tf, d), lambda i, f: (li, lj, f, 0)),
            pl.BlockSpec((1, d), lambda i, f: (0, 0)),
        ],
        out_specs=pl.BlockSpec((tm, d), lambda i, f: (i, 0)),
        out_shape=jax.ShapeDtypeStruct((t, d), F32),
        scratch_shapes=[pltpu.VMEM((tm, d), BF16)],
        compiler_params=_params("parallel", "arbitrary"),
        name="ffn",
    )(h, g.reshape(1, d), w_in, w_in, w_out, fg.reshape(1, d))


def _proj_body(x_ref, g_ref, w_ref, wg_ref, p_ref, gate_ref, u_ref):
    tm = x_ref.shape[0]
    rs = min(PROJ_ROWS, tm)

    def step(first):
        for r in range(tm // rs):
            rows = pl.ds(r * rs, rs)
            if first:
                u = _rmsnorm(x_ref[rows, :], g_ref[...]).astype(BF16)
                u_ref[rows, :] = u
                gate_ref[rows, :] = _dot(u, wg_ref[...])
            else:
                u = u_ref[rows, :]
            p_ref[rows, :] = _dot(u, w_ref[...]).astype(BF16)

    pl.when(pl.program_id(1) == 0)(lambda: step(True))
    pl.when(pl.program_id(1) > 0)(lambda: step(False))


def _proj(h, g, w, layer, n, wg):
    t, d = h.shape
    gw = wg.shape[1]
    tm, tn = _tile(t, PROJ_TM), _tile(n, PROJ_TN)
    assert t % tm == 0 and n % tn == 0
    return pl.pallas_call(
        _proj_body,
        grid=(t // tm, n // tn),
        in_specs=[
            pl.BlockSpec((tm, d), lambda i, j: (i, 0)),
            pl.BlockSpec((1, d), lambda i, j: (0, 0)),
            pl.BlockSpec((None, d, tn), lambda i, j: (layer, 0, j)),
            pl.BlockSpec((d, gw), lambda i, j: (0, 0)),
        ],
        out_specs=[
            pl.BlockSpec((tm, tn), lambda i, j: (i, j)),
            pl.BlockSpec((tm, gw), lambda i, j: (i, 0)),
        ],
        out_shape=[jax.ShapeDtypeStruct((t, n), BF16), jax.ShapeDtypeStruct((t, gw), F32)],
        scratch_shapes=[pltpu.VMEM((tm, d), BF16)],
        compiler_params=_params("parallel", "arbitrary"),
        name="mixer_in_proj",
    )(h, g.reshape(1, d), w, wg)


def _out_body(h_ref, y_ref, w_ref, o_ref):
    o_ref[...] = h_ref[...] + _dot(y_ref[...], w_ref[...])


def _out_proj(h, y, w, layer):
    t, d = h.shape
    k = y.shape[1]
    tm = min(OUT_TM, t)
    assert t % tm == 0
    return pl.pallas_call(
        _out_body,
        grid=(t // tm,),
        in_specs=[
            pl.BlockSpec((tm, d), lambda i: (i, 0)),
            pl.BlockSpec((tm, k), lambda i: (i, 0)),
            pl.BlockSpec((None, k, d), lambda i: (layer, 0, 0)),
        ],
        out_specs=pl.BlockSpec((tm, d), lambda i: (i, 0)),
        out_shape=jax.ShapeDtypeStruct((t, d), F32),
        compiler_params=_params("parallel"),
        name="mixer_out_proj",
    )(h, y, w)


def _mlstm_body(q_ref, k_ref, v_ref, o_ref, gate_ref, bias_ref, hn_ref, y_ref, c_ref, m_ref, *, heads):
    L = q_ref.shape[1]
    dqk, dv = MLSTM_QK, MLSTM_V
    scale = dqk ** -0.5

    @pl.when(pl.program_id(1) == 0)
    def _():
        c_ref[...] = jnp.zeros_like(c_ref)
        m_ref[...] = jnp.zeros_like(m_ref)

    gates = gate_ref[0] + bias_ref[...]
    i_pre = gates[:, :LANES]
    b_cum = _cumsum_rows(jax.nn.log_sigmoid(gates[:, LANES:]))
    m_all = m_ref[...]
    row = lax.broadcasted_iota(jnp.int32, (L, L), 0)
    col = lax.broadcasted_iota(jnp.int32, (L, L), 1)
    causal = col <= row
    ones_blk = jnp.ones((L, LANES), BF16)

    inter_log_all = b_cum + m_all
    b_last = b_cum[L - 1:L, :]
    g = b_last - b_cum + i_pre
    m_new = jnp.maximum(b_last + m_all, jnp.max(g, axis=0, keepdims=True))
    w_c_all = jnp.exp(b_last + m_all - m_new)
    w_s_all = jnp.exp(g - m_new) * scale
    m_ref[...] = m_new

    r_all = i_pre - b_cum
    r_t = r_all.T
    time = lax.broadcasted_iota(jnp.int32, (L, LANES), 0)
    run_max, step = r_all, 1
    while step < L:
        run_max = jnp.where(time >= step, jnp.maximum(run_max, pltpu.roll(run_max, step, 0)), run_max)
        step *= 2
    m_t_all = jnp.maximum(inter_log_all, b_cum + run_max)
    c_all = b_cum - m_t_all + math.log(scale)
    inter_w_all = jnp.exp(inter_log_all - m_t_all)
    floor_all = jnp.exp(-m_t_all)

    def gate_stage(h):
        col = slice(h, h + 1)
        d_w = jnp.where(causal, jnp.exp(c_all[:, col] + r_t[col, :]), 0.0)
        q = q_ref[0, :, h * dqk:(h + 1) * dqk]
        k = k_ref[0, :, h * dqk:(h + 1) * dqk]
        s = (_dot_nt(q, k) * d_w).astype(BF16)
        k_w = (k.astype(F32) * w_s_all[:, col]).astype(BF16)
        return q, s, inter_w_all[:, col], floor_all[:, col], w_c_all[:, col], k_w

    def value_stage(h, q, s, inter_w, floor, w_c, k_w):
        v_ext = jnp.concatenate([v_ref[0, :, h * dv:(h + 1) * dv], ones_blk], axis=1)
        c_prev = c_ref[h]
        nd = _dot(s, v_ext) + inter_w * _dot(q, c_prev.astype(BF16))
        num, den = nd[:, :dv], nd[:, dv:dv + 1]
        hh = num / jnp.maximum(jnp.abs(den), floor)
        hh = _rmsnorm(hh, hn_ref[:, h * dv:(h + 1) * dv])
        gate_o = jax.nn.sigmoid(o_ref[0, :, h * dv:(h + 1) * dv].astype(F32))
        y_ref[0, :, h * dv:(h + 1) * dv] = (hh * gate_o).astype(BF16)
        c_ref[h] = w_c * c_prev + _dot_tn(k_w, v_ext)

    staged = None
    for h in range(heads + 1):
        nxt = gate_stage(h) if h < heads else None
        if staged is not None:
            value_stage(h - 1, *staged)
        staged = nxt


def _mlstm_core(p, gates, bias, hn, heads):
    bsz, s, _ = p.shape
    nq, nv = heads * MLSTM_QK, heads * MLSTM_V
    L = min(MLSTM_CHUNK, s)
    assert s % L == 0 and nv == 2 * nq
    return pl.pallas_call(
        functools.partial(_mlstm_body, heads=heads),
        grid=(bsz, s // L),
        in_specs=[
            pl.BlockSpec((1, L, nq), lambda b, c: (b, c, 0)),
            pl.BlockSpec((1, L, nq), lambda b, c: (b, c, 1)),
            pl.BlockSpec((1, L, nv), lambda b, c: (b, c, 1)),
            pl.BlockSpec((1, L, nv), lambda b, c: (b, c, 2)),
            pl.BlockSpec((1, L, 2 * LANES), lambda b, c: (b, c, 0)),
            pl.BlockSpec((1, 2 * LANES), lambda b, c: (0, 0)),
            pl.BlockSpec((1, nv), lambda b, c: (0, 0)),
        ],
        out_specs=pl.BlockSpec((1, L, nv), lambda b, c: (b, c, 0)),
        out_shape=jax.ShapeDtypeStruct((bsz, s, nv), BF16),
        scratch_shapes=[pltpu.VMEM((heads, MLSTM_QK, MLSTM_V + LANES), F32), pltpu.VMEM((1, LANES), F32)],
        compiler_params=_params("parallel", "arbitrary"),
        name="mlstm_core",
    )(p, p, p, p, gates, bias, hn)


FOX_BIAS_PIECES = 3


def _fox_gate_body(gate_ref, bias_ref, cp_ref, carry_ref, *, heads):
    @pl.when(pl.program_id(1) == 0)
    def _():
        carry_ref[...] = jnp.zeros_like(carry_ref)

    log_f = jax.nn.log_sigmoid(gate_ref[0] + bias_ref[...])
    cum = _cumsum_rows(log_f) + carry_ref[...]
    tc = cum.shape[0]
    carry_ref[...] = cum[tc - 1:tc, :]
    lane = lax.broadcasted_iota(jnp.int32, (tc, LANES), 1)
    neg_log2e = -1.4426950408889634
    for h in range(heads):
        c = jnp.broadcast_to(cum[:, h:h + 1] * neg_log2e, (tc, LANES))
        hi, mid, lo = (x.astype(F32) for x in _split3(c))
        pieces = jnp.where(lane == 0, hi, jnp.where(lane == 1, mid, jnp.where(lane == 2, lo, 0.0)))
        cp_ref[0, h] = pieces.astype(BF16)


def _fox_gates(gates, bias, heads):
    bsz, s, _ = gates.shape
    tc = min(FOX_GATE_CHUNK, s)
    assert s % tc == 0
    return pl.pallas_call(
        functools.partial(_fox_gate_body, heads=heads),
        grid=(bsz, s // tc),
        in_specs=[
            pl.BlockSpec((1, tc, LANES), lambda b, c: (b, c, 0)),
            pl.BlockSpec((1, LANES), lambda b, c: (0, 0)),
        ],
        out_specs=pl.BlockSpec((1, heads, tc, LANES), lambda b, c: (b, 0, c, 0)),
        out_shape=jax.ShapeDtypeStruct((bsz, heads, s, LANES), BF16),
        scratch_shapes=[pltpu.VMEM((1, LANES), F32)],
        compiler_params=_params("parallel", "arbitrary"),
        name="fox_gates",
    )(gates, bias)


def _fox_attn_body(q_ref, k_ref, v_ref, cp_ref, y_ref, sa_ref, sb_ref, m_ref, l_ref, acc_ref):
    tq = q_ref.shape[1]
    hps = q_ref.shape[2] // FOX_HEAD
    qi = pl.program_id(2)
    q_scale = (FOX_HEAD ** -0.5) * 1.4426950408889634
    lane = lax.broadcasted_iota(jnp.int32, (tq, LANES), 1)
    ones_cols = jnp.where(lane < FOX_BIAS_PIECES, 1.0, 0.0).astype(BF16)
    heads = [pl.ds(g * FOX_HEAD, FOX_HEAD) for g in range(hps)]
    q_ext = [jnp.concatenate([(q_ref[0, :, cols].astype(F32) * q_scale).astype(BF16), ones_cols], axis=1)
             for cols in heads]
    m_ref[...] = jnp.full_like(m_ref, NEG_BIG)
    l_ref[...] = jnp.zeros_like(l_ref)
    acc_ref[...] = jnp.zeros_like(acc_ref)

    def scores(j, dst_ref, masked):
        rows = pl.ds(pl.multiple_of(j * tq, tq), tq)
        for g, cols in enumerate(heads):
            k_ext = jnp.concatenate([k_ref[0, rows, cols], cp_ref[0, g, rows, :]], axis=1)
            s = _dot_nt(k_ext, q_ext[g])
            if masked:
                key = lax.broadcasted_iota(jnp.int32, (tq, tq), 0)
                qry = lax.broadcasted_iota(jnp.int32, (tq, tq), 1)
                s = jnp.where(key <= qry, s, NEG_BIG)
            dst_ref[g] = s

    def consume(j, src_ref):
        rows = pl.ds(pl.multiple_of(j * tq, tq), tq)
        for g, cols in enumerate(heads):
            m_old = m_ref[g]
            m_new = jnp.maximum(m_old, jnp.max(src_ref[g], axis=0, keepdims=True))
            alpha = jnp.exp2(m_old - m_new)
            p = jnp.exp2(src_ref[g] - m_new)
            l_ref[g] = alpha * l_ref[g] + jnp.sum(p, axis=0, keepdims=True)
            acc_ref[g] = alpha * acc_ref[g] + _dot_tn(v_ref[0, rows, cols], p.astype(BF16))
            m_ref[g] = m_new

    one = jnp.int32(1)
    pl.when(qi == 0)(lambda: scores(0, sa_ref, True))
    pl.when(qi > 0)(lambda: scores(0, sa_ref, False))
    n_pairs = lax.shift_right_logical(jnp.maximum(qi - 1, 0), one)

    def pair_body(i, carry):
        j = 2 * i
        scores(j + 1, sb_ref, False)
        consume(j, sa_ref)
        scores(j + 2, sa_ref, False)
        consume(j + 1, sb_ref)
        return carry

    lax.fori_loop(0, n_pairs, pair_body, 0)

    @pl.when(qi == 0)
    def _():
        consume(0, sa_ref)

    @pl.when(lax.bitwise_and(qi, one) == 1)
    def _():
        scores(qi, sb_ref, True)
        consume(qi - 1, sa_ref)
        consume(qi, sb_ref)

    @pl.when(jnp.logical_and(qi > 0, lax.bitwise_and(qi, one) == 0))
    def _():
        scores(qi - 1, sb_ref, False)
        consume(qi - 2, sa_ref)
        scores(qi, sa_ref, True)
        consume(qi - 1, sb_ref)
        consume(qi, sa_ref)

    for g, cols in enumerate(heads):
        y_ref[0, :, cols] = (acc_ref[g] / l_ref[g]).T.astype(BF16)


def _fox_attn(p, cp, heads):
    bsz, s, _ = p.shape
    tq = min(FOX_TQ, s)
    hps = FOX_HEADS_PER_STEP
    assert s % tq == 0 and heads % hps == 0
    hg, w = heads // hps, hps * FOX_HEAD
    return pl.pallas_call(
        _fox_attn_body,
        grid=(bsz, hg, s // tq),
        in_specs=[
            pl.BlockSpec((1, tq, w), lambda b, h, i: (b, i, h)),
            pl.BlockSpec((1, s, w), lambda b, h, i: (b, 0, hg + h)),
            pl.BlockSpec((1, s, w), lambda b, h, i: (b, 0, 2 * hg + h)),
            pl.BlockSpec((1, hps, s, LANES), lambda b, h, i: (b, h, 0, 0)),
        ],
        out_specs=pl.BlockSpec((1, tq, w), lambda b, h, i: (b, i, h)),
        out_shape=jax.ShapeDtypeStruct((bsz, s, heads * FOX_HEAD), BF16),
        scratch_shapes=[pltpu.VMEM((hps, tq, tq), F32), pltpu.VMEM((hps, tq, tq), F32),
                        pltpu.VMEM((hps, 1, tq), F32), pltpu.VMEM((hps, 1, tq), F32),
                        pltpu.VMEM((hps, FOX_HEAD, tq), F32)],
        compiler_params=_params("parallel", "parallel", "arbitrary"),
        name="fox_attn",
    )(p, p, p, cp)


def _gla_intra_t(q, k, b):
    L = q.shape[0]
    sub = min(GLA_SUB, L)

    a_t = jnp.zeros((L, L), F32)
    if L > sub:
        dk = q.shape[1]
        zeros = lambda n: [jnp.zeros((n, dk), BF16)] if n else []
        ks, qs = [], []
        for n in range(sub, L, sub):
            r = b[n - 1:n, :]
            k_part = (k[:n, :] * jnp.exp(r - b[:n, :])).astype(BF16)
            q_part = (q[n:n + sub, :] * jnp.exp(b[n:n + sub, :] - r)).astype(BF16)
            ks.append(jnp.concatenate([k_part] + zeros(L - n), axis=0))
            qs.append(jnp.concatenate(zeros(n) + [q_part] + zeros(L - n - sub), axis=0))
        a_t = _dot_nt(jnp.concatenate(ks, axis=1), jnp.concatenate(qs, axis=1))

    strip_t = lax.broadcasted_iota(jnp.int32, (sub, L), 1)
    strips = []
    for lo in range(0, L, sub):
        strip = jnp.zeros((sub, L), F32)
        for t in range(lo, lo + sub):
            hi = 8 * (t // 8 + 1)
            s_idx = lo + lax.broadcasted_iota(jnp.int32, (hi - lo, 1), 0)
            diff = jnp.where(s_idx <= t, b[t:t + 1, :] - b[lo:hi, :], NEG_BIG)
            w = jnp.exp(diff) * (k[lo:hi, :] * q[t:t + 1, :])
            colsum = jnp.sum(w, axis=1, keepdims=True)
            if hi < lo + sub:
                colsum = jnp.concatenate([colsum, jnp.zeros((lo + sub - hi, 1), F32)], axis=0)
            strip = jnp.where(strip_t == t, colsum, strip)
        strips.append(strip)
    return a_t + jnp.concatenate(strips, axis=0)


def _gla_body(q_ref, k_ref, v_ref, r_ref, alow_ref, wal_ref, bal_ref, hn_ref, y_ref, st_ref, *, heads):
    L = q_ref.shape[1]
    dk, dv = GLA_K, GLA_V
    qscale = dk ** -0.5

    @pl.when(pl.program_id(1) == 0)
    def _():
        st_ref[...] = jnp.zeros_like(st_ref)

    log_alpha = jax.nn.log_sigmoid(_dot_f32(alow_ref[0], wal_ref[...]) + bal_ref[...]) * (1.0 / GLA_GATE_TEMP)
    b_all = _cumsum_rows(log_alpha)

    def decay_stage(h):
        b = b_all[:, h * dk:(h + 1) * dk]
        q = q_ref[0, :, h * dk:(h + 1) * dk].astype(F32) * qscale
        k = k_ref[0, :, h * dk:(h + 1) * dk].astype(F32)
        a_t = _gla_intra_t(q, k, b).astype(BF16)
        b_last = b[L - 1:L, :]
        q_dec = (q * jnp.exp(b)).astype(BF16)
        k_dec = (k * jnp.exp(b_last - b)).astype(BF16)
        return a_t, q_dec, k_dec, jnp.exp(b_last)

    def value_stage(h, a_t, q_dec, k_dec, chunk_decay):
        v = v_ref[0, :, h * dv:(h + 1) * dv]
        st = st_ref[h]
        o = _dot_tn(a_t, v) + _dot_nt(q_dec, st.astype(BF16))
        st_ref[h] = chunk_decay * st + _dot_tn(v, k_dec)
        o = _rmsnorm(o, hn_ref[:, h * dv:(h + 1) * dv])
        r = r_ref[0, :, h * dv:(h + 1) * dv].astype(F32)
        y_ref[0, :, h * dv:(h + 1) * dv] = (o * (r * jax.nn.sigmoid(r))).astype(BF16)

    staged = None
    for h in range(heads + 1):
        nxt = decay_stage(h) if h < heads else None
        if staged is not None:
            value_stage(h - 1, *staged)
        staged = nxt


def _gla_core(p, alow, w_alpha, b_alpha, hn, heads):
    bsz, s, _ = p.shape
    nk, nv = heads * GLA_K, heads * GLA_V
    L = min(GLA_CHUNK, s)
    assert s % L == 0 and nv == 2 * nk
    return pl.pallas_call(
        functools.partial(_gla_body, heads=heads),
        grid=(bsz, s // L),
        in_specs=[
            pl.BlockSpec((1, L, nk), lambda b, c: (b, c, 0)),
            pl.BlockSpec((1, L, nk), lambda b, c: (b, c, 1)),
            pl.BlockSpec((1, L, nv), lambda b, c: (b, c, 1)),
            pl.BlockSpec((1, L, nv), lambda b, c: (b, c, 2)),
            pl.BlockSpec((1, L, LANES), lambda b, c: (b, c, 0)),
            pl.BlockSpec((LANES, nk), lambda b, c: (0, 0)),
            pl.BlockSpec((1, nk), lambda b, c: (0, 0)),
            pl.BlockSpec((1, nv), lambda b, c: (0, 0)),
        ],
        out_specs=pl.BlockSpec((1, L, nv), lambda b, c: (b, c, 0)),
        out_shape=jax.ShapeDtypeStruct((bsz, s, nv), BF16),
        scratch_shapes=[pltpu.VMEM((heads, GLA_V, GLA_K), F32)],
        compiler_params=_params("parallel", "arbitrary"),
        name="gla_core",
    )(p, p, p, p, alow, w_alpha, b_alpha, hn)


def _pad_cols(w, width, offsets):
    out = jnp.zeros((w.shape[0], width), w.dtype)
    c0 = 0
    for off, n in offsets:
        out = out.at[:, off:off + n].set(w[:, c0:c0 + n])
        c0 += n
    return out


def _mlstm_layer(h, bsz, norm_g, w_in, w_in_bf, j, b_if, head_norm, w_out_bf):
    t, d = h.shape
    heads = d // MLSTM_V
    n_main = 2 * heads * MLSTM_QK + 2 * heads * MLSTM_V
    wg = _pad_cols(w_in[:, n_main:], 2 * LANES, [(0, heads), (LANES, heads)])
    bias = _pad_cols(b_if.reshape(1, -1), 2 * LANES, [(0, heads), (LANES, heads)])
    p, gates = _proj(h, norm_g, w_in_bf, j, n_main, wg.astype(BF16))
    y = _mlstm_core(p.reshape(bsz, t // bsz, n_main), gates.reshape(bsz, t // bsz, 2 * LANES), bias,
                    head_norm.reshape(1, -1), heads)
    return _out_proj(h, y.reshape(t, -1), w_out_bf, j)


def _fox_layer(h, bsz, norm_g, w_in, w_in_bf, j, b_f, w_out_bf):
    t, d = h.shape
    heads = d // FOX_HEAD
    n_main = 3 * heads * FOX_HEAD
    wg = _pad_cols(w_in[:, n_main:], LANES, [(0, heads)])
    bias = _pad_cols(b_f.reshape(1, -1), LANES, [(0, heads)])
    p, gates = _proj(h, norm_g, w_in_bf, j, n_main, wg.astype(BF16))
    s = t // bsz
    cp = _fox_gates(gates.reshape(bsz, s, LANES), bias, heads)
    y = _fox_attn(p.reshape(bsz, s, n_main), cp, heads)
    return _out_proj(h, y.reshape(t, -1), w_out_bf, j)


def _gla_layer(h, bsz, norm_g, w_in, w_in_bf, j, w_alpha, b_alpha, head_norm, w_out_bf):
    t, d = h.shape
    heads = d // GLA_V
    n_main = 2 * heads * GLA_K + 2 * heads * GLA_V
    rank = w_alpha.shape[0]
    wg = _pad_cols(w_in[:, n_main:], LANES, [(0, rank)])
    wal = jnp.zeros((LANES, w_alpha.shape[1]), F32).at[:rank].set(w_alpha)
    p, alow = _proj(h, norm_g, w_in_bf, j, n_main, wg.astype(BF16))
    s = t // bsz
    y = _gla_core(p.reshape(bsz, s, n_main), alow.reshape(bsz, s, LANES), wal, b_alpha.reshape(1, -1),
                  head_norm.reshape(1, -1), heads)
    return _out_proj(h, y.reshape(t, -1), w_out_bf, j)


def kernel(x, ffn_norm, ffn_w_in, ffn_w_out, mix_norm, a_w_in, a_b_if, a_head_norm, a_w_out, b_w_in, b_b_f, b_w_out, c_w_in, c_w_alpha, c_b_alpha, c_head_norm, c_w_out, final_norm):
    bsz, s, d = x.shape
    depth = ffn_norm.shape[0]
    h = x.reshape(bsz * s, d)
    ffn_in_bf, ffn_out_bf = ffn_w_in, ffn_w_out.astype(BF16)
    a_in_bf, a_out_bf = a_w_in.astype(BF16), a_w_out.astype(BF16)
    b_in_bf, b_out_bf = b_w_in.astype(BF16), b_w_out.astype(BF16)
    c_in_bf, c_out_bf = c_w_in.astype(BF16), c_w_out.astype(BF16)
    for i in range(depth):
        h = _ffn(h, ffn_norm[i, 0], ffn_in_bf, ffn_out_bf, (i, 0))
        kind, j = i % 3, i // 3
        if kind == 0:
            h = _mlstm_layer(h, bsz, mix_norm[i], a_w_in[j], a_in_bf, j, a_b_if[j], a_head_norm[j], a_out_bf)
        elif kind == 1:
            h = _fox_layer(h, bsz, mix_norm[i], b_w_in[j], b_in_bf, j, b_b_f[j], b_out_bf)
        else:
            h = _gla_layer(h, bsz, mix_norm[i], c_w_in[j], c_in_bf, j, c_w_alpha[j], c_b_alpha[j], c_head_norm[j],
                           c_out_bf)
        h = _ffn(h, ffn_norm[i, 1], ffn_in_bf, ffn_out_bf, (i, 1),
                 final_g=final_norm if i == depth - 1 else None)
    return h.reshape(bsz, s, d)
```

```python
import functools
import math

import jax
import jax.numpy as jnp
from jax import lax
from jax.experimental import pallas as pl
from jax.experimental.pallas import tpu as pltpu

F32 = jnp.float32
BF16 = jnp.bfloat16

RMS_EPS = 1e-6
FFN_RESIDUAL_WEIGHT = 0.5
GLA_GATE_TEMP = 16.0
NEG_BIG = -1e30

LANES = 128
MLSTM_QK, MLSTM_V = 128, 256
FOX_HEAD = 128
GLA_K, GLA_V = 256, 512

VMEM_LIMIT_BYTES = 62 * 1024 * 1024

FFN_TM, FFN_TF, FFN_ROWS = 1024, 512, 512
PROJ_TM, PROJ_TN, PROJ_ROWS = 1024, 2048, 512
OUT_TM = 512
MLSTM_CHUNK = 256
FOX_TQ = 512
FOX_HEADS_PER_STEP = 4
FOX_GATE_CHUNK = 512
GLA_CHUNK = 128
GLA_SUB = 16


def _params(*sem):
    return pltpu.CompilerParams(dimension_semantics=sem, vmem_limit_bytes=VMEM_LIMIT_BYTES)


def _tile(n, pref):
    if n <= pref:
        return n
    return max(c for c in range(LANES, pref + 1, LANES) if n % c == 0)


def _dot(a, b):
    return jnp.dot(a, b, preferred_element_type=F32)


def _dot_nt(a, b):
    return lax.dot_general(a, b, (((1,), (1,)), ((), ())), preferred_element_type=F32)


def _dot_tn(a, b):
    return lax.dot_general(a, b, (((0,), (0,)), ((), ())), preferred_element_type=F32)


def _split2(x):
    hi = x.astype(BF16)
    lo = (x - hi.astype(F32)).astype(BF16)
    return hi, lo


def _split3(x):
    hi = x.astype(BF16)
    r = x - hi.astype(F32)
    mid = r.astype(BF16)
    lo = (r - mid.astype(F32)).astype(BF16)
    return hi, mid, lo


def _dot_f32(a, b):
    ah, al = _split2(a)
    bh, bl = _split2(b)
    return _dot(ah, bh) + (_dot(ah, bl) + _dot(al, bh))


def _cumsum_rows(x):
    n = x.shape[0]
    tril = (lax.broadcasted_iota(jnp.int32, (n, n), 0) >= lax.broadcasted_iota(jnp.int32, (n, n), 1)).astype(BF16)
    hi, mid, lo = _split3(x)
    return _dot(tril, hi) + (_dot(tril, mid) + _dot(tril, lo))


def _rmsnorm(x, g):
    return x * lax.rsqrt(jnp.mean(x * x, axis=-1, keepdims=True) + RMS_EPS) * g


def _ffn_body(x_ref, g_ref, wa_ref, wb_ref, wo_ref, fg_ref, o_ref, u_ref, *, nf, final_norm):
    f = pl.program_id(1)
    tm = x_ref.shape[0]
    rs = min(FFN_ROWS, tm)
    row_tiles = [pl.ds(r * rs, rs) for r in range(tm // rs)]

    def step(first, last):
        for rows in row_tiles:
            if first:
                u = _rmsnorm(x_ref[rows, :], g_ref[...]).astype(BF16)
                u_ref[rows, :] = u
            else:
                u = u_ref[rows, :]
            a = _dot(u, wa_ref[...].astype(BF16))
            b = _dot(u, wb_ref[...].astype(BF16))
            hid = (a * jax.nn.sigmoid(a) * b).astype(BF16)
            acc = _dot(hid, wo_ref[...])
            if not first:
                acc = o_ref[rows, :] + acc
            if last:
                acc = x_ref[rows, :] + FFN_RESIDUAL_WEIGHT * acc
                if final_norm:
                    acc = _rmsnorm(acc, fg_ref[...])
            o_ref[rows, :] = acc

    if nf == 1:
        step(True, True)
    else:
        pl.when(f == 0)(lambda: step(True, False))
        if nf > 2:
            pl.when(jnp.logical_and(f > 0, f < nf - 1))(lambda: step(False, False))
        pl.when(f == nf - 1)(lambda: step(False, True))


def _ffn(h, g, w_in, w_out, layer, final_g=None):
    t, d = h.shape
    dff = w_out.shape[-2]
    tm, tf = _tile(t, FFN_TM), _tile(dff, FFN_TF)
    assert t % tm == 0 and dff % tf == 0
    nf = dff // tf
    fg = g if final_g is None else final_g
    li, lj = layer
    return pl.pallas_call(
        functools.partial(_ffn_body, nf=nf, final_norm=final_g is not None),
        grid=(t // tm, nf),
        in_specs=[
            pl.BlockSpec((tm, d), lambda i, f: (i, 0)),
            pl.BlockSpec((1, d), lambda i, f: (0, 0)),
            pl.BlockSpec((None, None, d, tf), lambda i, f: (li, lj, 0, f)),
            pl.BlockSpec((None, None, d, tf), lambda i, f: (li, lj, 0, f + nf)),
            pl.BlockSpec((None, None, tf, d), lambda i, f: (li, lj, f, 0)),
            pl.BlockSpec((1, d), lambda i, f: (0, 0)),
        ],
        out_specs=pl.BlockSpec((tm, d), lambda i, f: (i, 0)),
        out_shape=jax.ShapeDtypeStruct((t, d), F32),
        scratch_shapes=[pltpu.VMEM((tm, d), BF16)],
        compiler_params=_params("parallel", "arbitrary"),
        name="ffn",
    )(h, g.reshape(1, d), w_in, w_in, w_out, fg.reshape(1, d))


def _proj_body(x_ref, g_ref, w_ref, wg_ref, p_ref, gate_ref, u_ref):
    tm = x_ref.shape[0]
    rs = min(PROJ_ROWS, tm)

    def step(first):
        for r in range(tm // rs):
            rows = pl.ds(r * rs, rs)
            if first:
                u = _rmsnorm(x_ref[rows, :], g_ref[...]).astype(BF16)
                u_ref[rows, :] = u
                gate_ref[rows, :] = _dot(u, wg_ref[...])
            else:
                u = u_ref[rows, :]
            p_ref[rows, :] = _dot(u, w_ref[...]).astype(BF16)

    pl.when(pl.program_id(1) == 0)(lambda: step(True))
    pl.when(pl.program_id(1) > 0)(lambda: step(False))


def _proj(h, g, w, layer, n, wg):
    t, d = h.shape
    gw = wg.shape[1]
    tm, tn = _tile(t, PROJ_TM), _tile(n, PROJ_TN)
    assert t % tm == 0 and n % tn == 0
    return pl.pallas_call(
        _proj_body,
        grid=(t // tm, n // tn),
        in_specs=[
            pl.BlockSpec((tm, d), lambda i, j: (i, 0)),
            pl.BlockSpec((1, d), lambda i, j: (0, 0)),
            pl.BlockSpec((None, d, tn), lambda i, j: (layer, 0, j)),
            pl.BlockSpec((d, gw), lambda i, j: (0, 0)),
        ],
        out_specs=[
            pl.BlockSpec((tm, tn), lambda i, j: (i, j)),
            pl.BlockSpec((tm, gw), lambda i, j: (i, 0)),
        ],
        out_shape=[jax.ShapeDtypeStruct((t, n), BF16), jax.ShapeDtypeStruct((t, gw), F32)],
        scratch_shapes=[pltpu.VMEM((tm, d), BF16)],
        compiler_params=_params("parallel", "arbitrary"),
        name="mixer_in_proj",
    )(h, g.reshape(1, d), w, wg)


def _out_body(h_ref, y_ref, w_ref, o_ref):
    o_ref[...] = h_ref[...] + _dot(y_ref[...], w_ref[...])


def _out_proj(h, y, w, layer):
    t, d = h.shape
    k = y.shape[1]
    tm = min(OUT_TM, t)
    assert t % tm == 0
    return pl.pallas_call(
        _out_body,
        grid=(t // tm,),
        in_specs=[
            pl.BlockSpec((tm, d), lambda i: (i, 0)),
            pl.BlockSpec((tm, k), lambda i: (i, 0)),
            pl.BlockSpec((None, k, d), lambda i: (layer, 0, 0)),
        ],
        out_specs=pl.BlockSpec((tm, d), lambda i: (i, 0)),
        out_shape=jax.ShapeDtypeStruct((t, d), F32),
        compiler_params=_params("parallel"),
        name="mixer_out_proj",
    )(h, y, w)


def _mlstm_body(q_ref, k_ref, v_ref, o_ref, gate_ref, bias_ref, hn_ref, y_ref, c_ref, m_ref, *, heads):
    L = q_ref.shape[1]
    dqk, dv = MLSTM_QK, MLSTM_V
    scale = dqk ** -0.5

    @pl.when(pl.program_id(1) == 0)
    def _():
        c_ref[...] = jnp.zeros_like(c_ref)
        m_ref[...] = jnp.zeros_like(m_ref)

    gates = gate_ref[0] + bias_ref[...]
    i_pre = gates[:, :LANES]
    b_cum = _cumsum_rows(jax.nn.log_sigmoid(gates[:, LANES:]))
    m_all = m_ref[...]
    row = lax.broadcasted_iota(jnp.int32, (L, L), 0)
    col = lax.broadcasted_iota(jnp.int32, (L, L), 1)
    causal = col <= row
    ones_blk = jnp.ones((L, LANES), BF16)

    inter_log_all = b_cum + m_all
    b_last = b_cum[L - 1:L, :]
    g = b_last - b_cum + i_pre
    m_new = jnp.maximum(b_last + m_all, jnp.max(g, axis=0, keepdims=True))
    w_c_all = jnp.exp(b_last + m_all - m_new)
    w_s_all = jnp.exp(g - m_new) * scale
    m_ref[...] = m_new

    r_all = i_pre - b_cum
    r_t = r_all.T
    time = lax.broadcasted_iota(jnp.int32, (L, LANES), 0)
    run_max, step = r_all, 1
    while step < L:
        run_max = jnp.where(time >= step, jnp.maximum(run_max, pltpu.roll(run_max, step, 0)), run_max)
        step *= 2
    m_t_all = jnp.maximum(inter_log_all, b_cum + run_max)
    c_all = b_cum - m_t_all + math.log(scale)
    inter_w_all = jnp.exp(inter_log_all - m_t_all)
    floor_all = jnp.exp(-m_t_all)

    def gate_stage(h):
        col = slice(h, h + 1)
        d_w = jnp.where(causal, jnp.exp(c_all[:, col] + r_t[col, :]), 0.0)
        q = q_ref[0, :, h * dqk:(h + 1) * dqk]
        k = k_ref[0, :, h * dqk:(h + 1) * dqk]
        s = (_dot_nt(q, k) * d_w).astype(BF16)
        k_w = (k.astype(F32) * w_s_all[:, col]).astype(BF16)
        return q, s, inter_w_all[:, col], floor_all[:, col], w_c_all[:, col], k_w

    def value_stage(h, q, s, inter_w, floor, w_c, k_w):
        v_ext = jnp.concatenate([v_ref[0, :, h * dv:(h + 1) * dv], ones_blk], axis=1)
        c_prev = c_ref[h]
        nd = _dot(s, v_ext) + inter_w * _dot(q, c_prev.astype(BF16))
        c_ref[h] = w_c * c_prev + _dot_tn(k_w, v_ext)
        return nd, floor

    def output_stage(h, nd, floor):
        num, den = nd[:, :dv], nd[:, dv:dv + 1]
        hh = num / jnp.maximum(jnp.abs(den), floor)
        hh = _rmsnorm(hh, hn_ref[:, h * dv:(h + 1) * dv])
        gate_o = jax.nn.sigmoid(o_ref[0, :, h * dv:(h + 1) * dv].astype(F32))
        y_ref[0, :, h * dv:(h + 1) * dv] = (hh * gate_o).astype(BF16)

    gated, valued = None, None
    for h in range(heads + 2):
        nxt_gated = gate_stage(h) if h < heads else None
        nxt_valued = value_stage(h - 1, *gated) if gated is not None else None
        if valued is not None:
            output_stage(h - 2, *valued)
        gated, valued = nxt_gated, nxt_valued


def _mlstm_core(p, gates, bias, hn, heads):
    bsz, s, _ = p.shape
    nq, nv = heads * MLSTM_QK, heads * MLSTM_V
    L = min(MLSTM_CHUNK, s)
    assert s % L == 0 and nv == 2 * nq
    return pl.pallas_call(
        functools.partial(_mlstm_body, heads=heads),
        grid=(bsz, s // L),
        in_specs=[
            pl.BlockSpec((1, L, nq), lambda b, c: (b, c, 0)),
            pl.BlockSpec((1, L, nq), lambda b, c: (b, c, 1)),
            pl.BlockSpec((1, L, nv), lambda b, c: (b, c, 1)),
            pl.BlockSpec((1, L, nv), lambda b, c: (b, c, 2)),
            pl.BlockSpec((1, L, 2 * LANES), lambda b, c: (b, c, 0)),
            pl.BlockSpec((1, 2 * LANES), lambda b, c: (0, 0)),
            pl.BlockSpec((1, nv), lambda b, c: (0, 0)),
        ],
        out_specs=pl.BlockSpec((1, L, nv), lambda b, c: (b, c, 0)),
        out_shape=jax.ShapeDtypeStruct((bsz, s, nv), BF16),
        scratch_shapes=[pltpu.VMEM((heads, MLSTM_QK, MLSTM_V + LANES), F32), pltpu.VMEM((1, LANES), F32)],
        compiler_params=_params("parallel", "arbitrary"),
        name="mlstm_core",
    )(p, p, p, p, gates, bias, hn)


FOX_BIAS_PIECES = 3


def _fox_gate_body(gate_ref, bias_ref, cp_ref, carry_ref, *, heads):
    @pl.when(pl.program_id(1) == 0)
    def _():
        carry_ref[...] = jnp.zeros_like(carry_ref)

    log_f = jax.nn.log_sigmoid(gate_ref[0] + bias_ref[...])
    cum = _cumsum_rows(log_f) + carry_ref[...]
    tc = cum.shape[0]
    carry_ref[...] = cum[tc - 1:tc, :]
    lane = lax.broadcasted_iota(jnp.int32, (tc, LANES), 1)
    neg_log2e = -1.4426950408889634
    for h in range(heads):
        c = jnp.broadcast_to(cum[:, h:h + 1] * neg_log2e, (tc, LANES))
        hi, mid, lo = (x.astype(F32) for x in _split3(c))
        pieces = jnp.where(lane == 0, hi, jnp.where(lane == 1, mid, jnp.where(lane == 2, lo, 0.0)))
        cp_ref[0, h] = pieces.astype(BF16)


def _fox_gates(gates, bias, heads):
    bsz, s, _ = gates.shape
    tc = min(FOX_GATE_CHUNK, s)
    assert s % tc == 0
    return pl.pallas_call(
        functools.partial(_fox_gate_body, heads=heads),
        grid=(bsz, s // tc),
        in_specs=[
            pl.BlockSpec((1, tc, LANES), lambda b, c: (b, c, 0)),
            pl.BlockSpec((1, LANES), lambda b, c: (0, 0)),
        ],
        out_specs=pl.BlockSpec((1, heads, tc, LANES), lambda b, c: (b, 0, c, 0)),
        out_shape=jax.ShapeDtypeStruct((bsz, heads, s, LANES), BF16),
        scratch_shapes=[pltpu.VMEM((1, LANES), F32)],
        compiler_params=_params("parallel", "arbitrary"),
        name="fox_gates",
    )(gates, bias)


def _fox_attn_body(q_ref, k_ref, v_ref, cp_ref, y_ref, sa_ref, sb_ref, m_ref, l_ref, acc_ref):
    tq = q_ref.shape[1]
    hps = q_ref.shape[2] // FOX_HEAD
    qi = pl.program_id(2)
    q_scale = (FOX_HEAD ** -0.5) * 1.4426950408889634
    lane = lax.broadcasted_iota(jnp.int32, (tq, LANES), 1)
    ones_cols = jnp.where(lane < FOX_BIAS_PIECES, 1.0, 0.0).astype(BF16)
    heads = [pl.ds(g * FOX_HEAD, FOX_HEAD) for g in range(hps)]
    q_ext = [jnp.concatenate([(q_ref[0, :, cols].astype(F32) * q_scale).astype(BF16), ones_cols], axis=1)
             for cols in heads]
    m_ref[...] = jnp.full_like(m_ref, NEG_BIG)
    l_ref[...] = jnp.zeros_like(l_ref)
    acc_ref[...] = jnp.zeros_like(acc_ref)

    def score_head(g, j, dst_ref, masked):
        rows = pl.ds(pl.multiple_of(j * tq, tq), tq)
        k_ext = jnp.concatenate([k_ref[0, rows, heads[g]], cp_ref[0, g, rows, :]], axis=1)
        s = _dot_nt(k_ext, q_ext[g])
        if masked:
            key = lax.broadcasted_iota(jnp.int32, (tq, tq), 0)
            qry = lax.broadcasted_iota(jnp.int32, (tq, tq), 1)
            s = jnp.where(key <= qry, s, NEG_BIG)
        dst_ref[g] = s

    def consume_head(g, j, src_ref):
        rows = pl.ds(pl.multiple_of(j * tq, tq), tq)
        m_old = m_ref[g]
        m_new = jnp.maximum(m_old, jnp.max(src_ref[g], axis=0, keepdims=True))
        alpha = jnp.exp2(m_old - m_new)
        p = jnp.exp2(src_ref[g] - m_new)
        l_ref[g] = alpha * l_ref[g] + jnp.sum(p, axis=0, keepdims=True)
        acc_ref[g] = alpha * acc_ref[g] + _dot_tn(v_ref[0, rows, heads[g]], p.astype(BF16))
        m_ref[g] = m_new

    def scores(j, dst_ref, masked):
        for g in range(hps):
            score_head(g, j, dst_ref, masked)

    def consume(j, src_ref):
        for g in range(hps):
            consume_head(g, j, src_ref)

    def advance(j_next, dst_ref, masked, j_cur, src_ref):
        for g in range(hps):
            score_head(g, j_next, dst_ref, masked)
            consume_head(g, j_cur, src_ref)

    one = jnp.int32(1)
    pl.when(qi == 0)(lambda: scores(0, sa_ref, True))
    pl.when(qi > 0)(lambda: scores(0, sa_ref, False))
    n_pairs = lax.shift_right_logical(jnp.maximum(qi - 1, 0), one)

    def pair_body(i, carry):
        j = 2 * i
        advance(j + 1, sb_ref, False, j, sa_ref)
        advance(j + 2, sa_ref, False, j + 1, sb_ref)
        return carry

    lax.fori_loop(0, n_pairs, pair_body, 0)

    @pl.when(qi == 0)
    def _():
        consume(0, sa_ref)

    @pl.when(lax.bitwise_and(qi, one) == 1)
    def _():
        advance(qi, sb_ref, True, qi - 1, sa_ref)
        consume(qi, sb_ref)

    @pl.when(jnp.logical_and(qi > 0, lax.bitwise_and(qi, one) == 0))
    def _():
        advance(qi - 1, sb_ref, False, qi - 2, sa_ref)
        advance(qi, sa_ref, True, qi - 1, sb_ref)
        consume(qi, sa_ref)

    for g, cols in enumerate(heads):
        y_ref[0, :, cols] = (acc_ref[g] / l_ref[g]).T.astype(BF16)


def _fox_attn(p, cp, heads):
    bsz, s, _ = p.shape
    tq = min(FOX_TQ, s)
    hps = FOX_HEADS_PER_STEP
    assert s % tq == 0 and heads % hps == 0
    hg, w = heads // hps, hps * FOX_HEAD
    return pl.pallas_call(
        _fox_attn_body,
        grid=(bsz, hg, s // tq),
        in_specs=[
            pl.BlockSpec((1, tq, w), lambda b, h, i: (b, i, h)),
            pl.BlockSpec((1, s, w), lambda b, h, i: (b, 0, hg + h)),
            pl.BlockSpec((1, s, w), lambda b, h, i: (b, 0, 2 * hg + h)),
            pl.BlockSpec((1, hps, s, LANES), lambda b, h, i: (b, h, 0, 0)),
        ],
        out_specs=pl.BlockSpec((1, tq, w), lambda b, h, i: (b, i, h)),
        out_shape=jax.ShapeDtypeStruct((bsz, s, heads * FOX_HEAD), BF16),
        scratch_shapes=[pltpu.VMEM((hps, tq, tq), F32), pltpu.VMEM((hps, tq, tq), F32),
                        pltpu.VMEM((hps, 1, tq), F32), pltpu.VMEM((hps, 1, tq), F32),
                        pltpu.VMEM((hps, FOX_HEAD, tq), F32)],
        compiler_params=_params("parallel", "parallel", "arbitrary"),
        name="fox_attn",
    )(p, p, p, cp)


def _gla_intra_t(q, k, b):
    L = q.shape[0]
    sub = min(GLA_SUB, L)

    a_t = jnp.zeros((L, L), F32)
    if L > sub:
        dk = q.shape[1]
        zeros = lambda n: [jnp.zeros((n, dk), BF16)] if n else []
        ks, qs = [], []
        for n in range(sub, L, sub):
            r = b[n - 1:n, :]
            k_part = (k[:n, :] * jnp.exp(r - b[:n, :])).astype(BF16)
            q_part = (q[n:n + sub, :] * jnp.exp(b[n:n + sub, :] - r)).astype(BF16)
            ks.append(jnp.concatenate([k_part] + zeros(L - n), axis=0))
            qs.append(jnp.concatenate(zeros(n) + [q_part] + zeros(L - n - sub), axis=0))
        a_t = _dot_nt(jnp.concatenate(ks, axis=1), jnp.concatenate(qs, axis=1))

    strip_t = lax.broadcasted_iota(jnp.int32, (sub, L), 1)
    strips = []
    for lo in range(0, L, sub):
        strip = jnp.zeros((sub, L), F32)
        for t in range(lo, lo + sub):
            hi = 8 * (t // 8 + 1)
            s_idx = lo + lax.broadcasted_iota(jnp.int32, (hi - lo, 1), 0)
            diff = jnp.where(s_idx <= t, b[t:t + 1, :] - b[lo:hi, :], NEG_BIG)
            w = jnp.exp(diff) * (k[lo:hi, :] * q[t:t + 1, :])
            colsum = jnp.sum(w, axis=1, keepdims=True)
            if hi < lo + sub:
                colsum = jnp.concatenate([colsum, jnp.zeros((lo + sub - hi, 1), F32)], axis=0)
            strip = jnp.where(strip_t == t, colsum, strip)
        strips.append(strip)
    return a_t + jnp.concatenate(strips, axis=0)


def _gla_body(q_ref, k_ref, v_ref, r_ref, alow_ref, wal_ref, bal_ref, hn_ref, y_ref, st_ref, *, heads):
    L = q_ref.shape[1]
    dk, dv = GLA_K, GLA_V
    qscale = dk ** -0.5

    @pl.when(pl.program_id(1) == 0)
    def _():
        st_ref[...] = jnp.zeros_like(st_ref)

    log_alpha = jax.nn.log_sigmoid(_dot_f32(alow_ref[0], wal_ref[...]) + bal_ref[...]) * (1.0 / GLA_GATE_TEMP)
    b_all = _cumsum_rows(log_alpha)

    def decay_stage(h):
        b = b_all[:, h * dk:(h + 1) * dk]
        q = q_ref[0, :, h * dk:(h + 1) * dk].astype(F32) * qscale
        k = k_ref[0, :, h * dk:(h + 1) * dk].astype(F32)
        a_t = _gla_intra_t(q, k, b).astype(BF16)
        b_last = b[L - 1:L, :]
        q_dec = (q * jnp.exp(b)).astype(BF16)
        k_dec = (k * jnp.exp(b_last - b)).astype(BF16)
        return a_t, q_dec, k_dec, jnp.exp(b_last)

    def value_stage(h, a_t, q_dec, k_dec, chunk_decay):
        v = v_ref[0, :, h * dv:(h + 1) * dv]
        st = st_ref[h]
        o = _dot_tn(a_t, v) + _dot_nt(q_dec, st.astype(BF16))
        st_ref[h] = chunk_decay * st + _dot_tn(v, k_dec)
        return (o,)

    def output_stage(h, o):
        o = _rmsnorm(o, hn_ref[:, h * dv:(h + 1) * dv])
        r = r_ref[0, :, h * dv:(h + 1) * dv].astype(F32)
        y_ref[0, :, h * dv:(h + 1) * dv] = (o * (r * jax.nn.sigmoid(r))).astype(BF16)

    decayed, valued = None, None
    for h in range(heads + 2):
        nxt_decayed = decay_stage(h) if h < heads else None
        nxt_valued = value_stage(h - 1, *decayed) if decayed is not None else None
        if valued is not None:
            output_stage(h - 2, *valued)
        decayed, valued = nxt_decayed, nxt_valued


def _gla_core(p, alow, w_alpha, b_alpha, hn, heads):
    bsz, s, _ = p.shape
    nk, nv = heads * GLA_K, heads * GLA_V
    L = min(GLA_CHUNK, s)
    assert s % L == 0 and nv == 2 * nk
    return pl.pallas_call(
        functools.partial(_gla_body, heads=heads),
        grid=(bsz, s // L),
        in_specs=[
            pl.BlockSpec((1, L, nk), lambda b, c: (b, c, 0)),
            pl.BlockSpec((1, L, nk), lambda b, c: (b, c, 1)),
            pl.BlockSpec((1, L, nv), lambda b, c: (b, c, 1)),
            pl.BlockSpec((1, L, nv), lambda b, c: (b, c, 2)),
            pl.BlockSpec((1, L, LANES), lambda b, c: (b, c, 0)),
            pl.BlockSpec((LANES, nk), lambda b, c: (0, 0)),
            pl.BlockSpec((1, nk), lambda b, c: (0, 0)),
            pl.BlockSpec((1, nv), lambda b, c: (0, 0)),
        ],
        out_specs=pl.BlockSpec((1, L, nv), lambda b, c: (b, c, 0)),
        out_shape=jax.ShapeDtypeStruct((bsz, s, nv), BF16),
        scratch_shapes=[pltpu.VMEM((heads, GLA_V, GLA_K), F32)],
        compiler_params=_params("parallel", "arbitrary"),
        name="gla_core",
    )(p, p, p, p, alow, w_alpha, b_alpha, hn)


def _pad_cols(w, width, offsets):
    out = jnp.zeros((w.shape[0], width), w.dtype)
    c0 = 0
    for off, n in offsets:
        out = out.at[:, off:off + n].set(w[:, c0:c0 + n])
        c0 += n
    return out


def _mlstm_layer(h, bsz, norm_g, w_in, w_in_bf, j, b_if, head_norm, w_out_bf):
    t, d = h.shape
    heads = d // MLSTM_V
    n_main = 2 * heads * MLSTM_QK + 2 * heads * MLSTM_V
    wg = _pad_cols(w_in[:, n_main:], 2 * LANES, [(0, heads), (LANES, heads)])
    bias = _pad_cols(b_if.reshape(1, -1), 2 * LANES, [(0, heads), (LANES, heads)])
    p, gates = _proj(h, norm_g, w_in_bf, j, n_main, wg.astype(BF16))
    y = _mlstm_core(p.reshape(bsz, t // bsz, n_main), gates.reshape(bsz, t // bsz, 2 * LANES), bias,
                    head_norm.reshape(1, -1), heads)
    return _out_proj(h, y.reshape(t, -1), w_out_bf, j)


def _fox_layer(h, bsz, norm_g, w_in, w_in_bf, j, b_f, w_out_bf):
    t, d = h.shape
    heads = d // FOX_HEAD
    n_main = 3 * heads * FOX_HEAD
    wg = _pad_cols(w_in[:, n_main:], LANES, [(0, heads)])
    bias = _pad_cols(b_f.reshape(1, -1), LANES, [(0, heads)])
    p, gates = _proj(h, norm_g, w_in_bf, j, n_main, wg.astype(BF16))
    s = t // bsz
    cp = _fox_gates(gates.reshape(bsz, s, LANES), bias, heads)
    y = _fox_attn(p.reshape(bsz, s, n_main), cp, heads)
    return _out_proj(h, y.reshape(t, -1), w_out_bf, j)


def _gla_layer(h, bsz, norm_g, w_in, w_in_bf, j, w_alpha, b_alpha, head_norm, w_out_bf):
    t, d = h.shape
    heads = d // GLA_V
    n_main = 2 * heads * GLA_K + 2 * heads * GLA_V
    rank = w_alpha.shape[0]
    wg = _pad_cols(w_in[:, n_main:], LANES, [(0, rank)])
    wal = jnp.zeros((LANES, w_alpha.shape[1]), F32).at[:rank].set(w_alpha)
    p, alow = _proj(h, norm_g, w_in_bf, j, n_main, wg.astype(BF16))
    s = t // bsz
    y = _gla_core(p.reshape(bsz, s, n_main), alow.reshape(bsz, s, LANES), wal, b_alpha.reshape(1, -1),
                  head_norm.reshape(1, -1), heads)
    return _out_proj(h, y.reshape(t, -1), w_out_bf, j)


def kernel(x, ffn_norm, ffn_w_in, ffn_w_out, mix_norm, a_w_in, a_b_if, a_head_norm, a_w_out, b_w_in, b_b_f, b_w_out, c_w_in, c_w_alpha, c_b_alpha, c_head_norm, c_w_out, final_norm):
    bsz, s, d = x.shape
    depth = ffn_norm.shape[0]
    h = x.reshape(bsz * s, d)
    ffn_in_bf, ffn_out_bf = ffn_w_in, ffn_w_out.astype(BF16)
    a_in_bf, a_out_bf = a_w_in.astype(BF16), a_w_out.astype(BF16)
    b_in_bf, b_out_bf = b_w_in.astype(BF16), b_w_out.astype(BF16)
    c_in_bf, c_out_bf = c_w_in.astype(BF16), c_w_out.astype(BF16)
    for i in range(depth):
        h = _ffn(h, ffn_norm[i, 0], ffn_in_bf, ffn_out_bf, (i, 0))
        kind, j = i % 3, i // 3
        if kind == 0:
            h = _mlstm_layer(h, bsz, mix_norm[i], a_w_in[j], a_in_bf, j, a_b_if[j], a_head_norm[j], a_out_bf)
        elif kind == 1:
            h = _fox_layer(h, bsz, mix_norm[i], b_w_in[j], b_in_bf, j, b_b_f[j], b_out_bf)
        else:
            h = _gla_layer(h, bsz, mix_norm[i], c_w_in[j], c_in_bf, j, c_w_alpha[j], c_b_alpha[j], c_head_norm[j],
                           c_out_bf)
        h = _ffn(h, ffn_norm[i, 1], ffn_in_bf, ffn_out_bf, (i, 1),
                 final_g=final_norm if i == depth - 1 else None)
    return h.reshape(bsz, s, d)
```

```python
import functools
import math

import jax
import jax.numpy as jnp
from jax import lax
from jax.experimental import pallas as pl
from jax.experimental.pallas import tpu as pltpu

F32 = jnp.float32
BF16 = jnp.bfloat16

RMS_EPS = 1e-6
FFN_RESIDUAL_WEIGHT = 0.5
GLA_GATE_TEMP = 16.0
NEG_BIG = -1e30

LANES = 128
MLSTM_QK, MLSTM_V = 128, 256
FOX_HEAD = 128
GLA_K, GLA_V = 256, 512

VMEM_LIMIT_BYTES = 62 * 1024 * 1024

FFN_TM, FFN_TF, FFN_ROWS = 1024, 512, 512
PROJ_TM, PROJ_TN, PROJ_ROWS = 1024, 2048, 512
OUT_TM = 1024
MLSTM_CHUNK = 256
FOX_TQ = 512
FOX_HEADS_PER_STEP = 4
FOX_GATE_CHUNK = 512
GLA_CHUNK = 128
GLA_SUB = 16


def _params(*sem):
    return pltpu.CompilerParams(dimension_semantics=sem, vmem_limit_bytes=VMEM_LIMIT_BYTES)


def _tile(n, pref):
    if n <= pref:
        return n
    return max(c for c in range(LANES, pref + 1, LANES) if n % c == 0)


def _dot(a, b):
    return jnp.dot(a, b, preferred_element_type=F32)


def _dot_nt(a, b):
    return lax.dot_general(a, b, (((1,), (1,)), ((), ())), preferred_element_type=F32)


def _dot_tn(a, b):
    return lax.dot_general(a, b, (((0,), (0,)), ((), ())), preferred_element_type=F32)


def _split2(x):
    hi = x.astype(BF16)
    lo = (x - hi.astype(F32)).astype(BF16)
    return hi, lo


def _split3(x):
    hi = x.astype(BF16)
    r = x - hi.astype(F32)
    mid = r.astype(BF16)
    lo = (r - mid.astype(F32)).astype(BF16)
    return hi, mid, lo


def _dot_f32(a, b):
    ah, al = _split2(a)
    bh, bl = _split2(b)
    return _dot(ah, bh) + (_dot(ah, bl) + _dot(al, bh))


def _cumsum_rows(x):
    n = x.shape[0]
    tril = (lax.broadcasted_iota(jnp.int32, (n, n), 0) >= lax.broadcasted_iota(jnp.int32, (n, n), 1)).astype(BF16)
    hi, mid, lo = _split3(x)
    return _dot(tril, hi) + (_dot(tril, mid) + _dot(tril, lo))


def _rmsnorm(x, g):
    return x * lax.rsqrt(jnp.mean(x * x, axis=-1, keepdims=True) + RMS_EPS) * g


def _ffn_body(x_ref, g_ref, wa_ref, wb_ref, wo_ref, fg_ref, o_ref, u_ref, *, nf, final_norm):
    f = pl.program_id(1)
    tm = x_ref.shape[0]
    rs = min(FFN_ROWS, tm)
    row_tiles = [pl.ds(r * rs, rs) for r in range(tm // rs)]

    def step(first, last):
        for rows in row_tiles:
            if first:
                u = _rmsnorm(x_ref[rows, :], g_ref[...]).astype(BF16)
                u_ref[rows, :] = u
            else:
                u = u_ref[rows, :]
            a = _dot(u, wa_ref[...].astype(BF16))
            b = _dot(u, wb_ref[...].astype(BF16))
            hid = (a * jax.nn.sigmoid(a) * b).astype(BF16)
            acc = _dot(hid, wo_ref[...])
            if not first:
                acc = o_ref[rows, :] + acc
            if last:
                acc = x_ref[rows, :] + FFN_RESIDUAL_WEIGHT * acc
                if final_norm:
                    acc = _rmsnorm(acc, fg_ref[...])
            o_ref[rows, :] = acc

    if nf == 1:
        step(True, True)
    else:
        pl.when(f == 0)(lambda: step(True, False))
        if nf > 2:
            pl.when(jnp.logical_and(f > 0, f < nf - 1))(lambda: step(False, False))
        pl.when(f == nf - 1)(lambda: step(False, True))


def _ffn(h, g, w_in, w_out, layer, final_g=None):
    t, d = h.shape
    dff = w_out.shape[-2]
    tm, tf = _tile(t, FFN_TM), _tile(dff, FFN_TF)
    assert t % tm == 0 and dff % tf == 0
    nf = dff // tf
    fg = g if final_g is None else final_g
    li, lj = layer
    return pl.pallas_call(
        functools.partial(_ffn_body, nf=nf, final_norm=final_g is not None),
        grid=(t // tm, nf),
        in_specs=[
            pl.BlockSpec((tm, d), lambda i, f: (i, 0)),
            pl.BlockSpec((1, d), lambda i, f: (0, 0)),
            pl.BlockSpec((None, None, d, tf), lambda i, f: (li, lj, 0, f)),
            pl.BlockSpec((None, None, d, tf), lambda i, f: (li, lj, 0, f + nf)),
            pl.BlockSpec((None, None, tf, d), lambda i, f: (li, lj, f, 0)),
            pl.BlockSpec((1, d), lambda i, f: (0, 0)),
        ],
        out_specs=pl.BlockSpec((tm, d), lambda i, f: (i, 0)),
        out_shape=jax.ShapeDtypeStruct((t, d), F32),
        scratch_shapes=[pltpu.VMEM((tm, d), BF16)],
        compiler_params=_params("parallel", "arbitrary"),
        name="ffn",
    )(h, g.reshape(1, d), w_in, w_in, w_out, fg.reshape(1, d))


def _proj_body(x_ref, g_ref, w_ref, wg_ref, p_ref, gate_ref, u_ref):
    tm = x_ref.shape[0]
    rs = min(PROJ_ROWS, tm)

    def step(first):
        for r in range(tm // rs):
            rows = pl.ds(r * rs, rs)
            if first:
                u = _rmsnorm(x_ref[rows, :], g_ref[...]).astype(BF16)
                u_ref[rows, :] = u
                gate_ref[rows, :] = _dot(u, wg_ref[...])
            else:
                u = u_ref[rows, :]
            p_ref[rows, :] = _dot(u, w_ref[...]).astype(BF16)

    pl.when(pl.program_id(1) == 0)(lambda: step(True))
    pl.when(pl.program_id(1) > 0)(lambda: step(False))


def _proj(h, g, w, layer, n, wg):
    t, d = h.shape
    gw = wg.shape[1]
    tm, tn = _tile(t, PROJ_TM), _tile(n, PROJ_TN)
    assert t % tm == 0 and n % tn == 0
    return pl.pallas_call(
        _proj_body,
        grid=(t // tm, n // tn),
        in_specs=[
            pl.BlockSpec((tm, d), lambda i, j: (i, 0)),
            pl.BlockSpec((1, d), lambda i, j: (0, 0)),
            pl.BlockSpec((None, d, tn), lambda i, j: (layer, 0, j)),
            pl.BlockSpec((d, gw), lambda i, j: (0, 0)),
        ],
        out_specs=[
            pl.BlockSpec((tm, tn), lambda i, j: (i, j)),
            pl.BlockSpec((tm, gw), lambda i, j: (i, 0)),
        ],
        out_shape=[jax.ShapeDtypeStruct((t, n), BF16), jax.ShapeDtypeStruct((t, gw), F32)],
        scratch_shapes=[pltpu.VMEM((tm, d), BF16)],
        compiler_params=_params("parallel", "arbitrary"),
        name="mixer_in_proj",
    )(h, g.reshape(1, d), w, wg)


def _out_body(h_ref, y_ref, w_ref, o_ref):
    tm = h_ref.shape[0]
    rs = min(PROJ_ROWS, tm)
    for r in range(tm // rs):
        rows = pl.ds(r * rs, rs)
        o_ref[rows, :] = h_ref[rows, :] + _dot(y_ref[rows, :], w_ref[...])


def _out_proj(h, y, w, layer):
    t, d = h.shape
    k = y.shape[1]
    tm = min(OUT_TM, t)
    assert t % tm == 0
    return pl.pallas_call(
        _out_body,
        grid=(t // tm,),
        in_specs=[
            pl.BlockSpec((tm, d), lambda i: (i, 0)),
            pl.BlockSpec((tm, k), lambda i: (i, 0)),
            pl.BlockSpec((None, k, d), lambda i: (layer, 0, 0), pipeline_mode=pl.Buffered(1)),
        ],
        out_specs=pl.BlockSpec((tm, d), lambda i: (i, 0)),
        out_shape=jax.ShapeDtypeStruct((t, d), F32),
        compiler_params=_params("parallel"),
        name="mixer_out_proj",
    )(h, y, w)


def _mlstm_body(q_ref, k_ref, v_ref, o_ref, gate_ref, bias_ref, hn_ref, y_ref, c_ref, m_ref, *, heads):
    L = q_ref.shape[1]
    dqk, dv = MLSTM_QK, MLSTM_V
    scale = dqk ** -0.5

    @pl.when(pl.program_id(1) == 0)
    def _():
        c_ref[...] = jnp.zeros_like(c_ref)
        m_ref[...] = jnp.zeros_like(m_ref)

    gates = gate_ref[0] + bias_ref[...]
    i_pre = gates[:, :LANES]
    b_cum = _cumsum_rows(jax.nn.log_sigmoid(gates[:, LANES:]))
    m_all = m_ref[...]
    row = lax.broadcasted_iota(jnp.int32, (L, L), 0)
    col = lax.broadcasted_iota(jnp.int32, (L, L), 1)
    causal = col <= row
    ones_blk = jnp.ones((L, LANES), BF16)

    inter_log_all = b_cum + m_all
    b_last = b_cum[L - 1:L, :]
    g = b_last - b_cum + i_pre
    m_new = jnp.maximum(b_last + m_all, jnp.max(g, axis=0, keepdims=True))
    w_c_all = jnp.exp(b_last + m_all - m_new)
    w_s_all = jnp.exp(g - m_new) * scale
    m_ref[...] = m_new

    r_all = i_pre - b_cum
    r_t = r_all.T
    time = lax.broadcasted_iota(jnp.int32, (L, LANES), 0)
    run_max, step = r_all, 1
    while step < L:
        run_max = jnp.where(time >= step, jnp.maximum(run_max, pltpu.roll(run_max, step, 0)), run_max)
        step *= 2
    m_t_all = jnp.maximum(inter_log_all, b_cum + run_max)
    c_all = b_cum - m_t_all + math.log(scale)
    inter_w_all = jnp.exp(inter_log_all - m_t_all)
    floor_all = jnp.exp(-m_t_all)

    def gate_stage(h):
        col = slice(h, h + 1)
        d_w = jnp.where(causal, jnp.exp(c_all[:, col] + r_t[col, :]), 0.0)
        q = q_ref[0, :, h * dqk:(h + 1) * dqk]
        k = k_ref[0, :, h * dqk:(h + 1) * dqk]
        s = (_dot_nt(q, k) * d_w).astype(BF16)
        k_w = (k.astype(F32) * w_s_all[:, col]).astype(BF16)
        return q, s, inter_w_all[:, col], floor_all[:, col], w_c_all[:, col], k_w

    def value_stage(h, q, s, inter_w, floor, w_c, k_w):
        v_ext = jnp.concatenate([v_ref[0, :, h * dv:(h + 1) * dv], ones_blk], axis=1)
        c_prev = c_ref[h]
        nd = _dot(s, v_ext) + inter_w * _dot(q, c_prev.astype(BF16))
        c_ref[h] = w_c * c_prev + _dot_tn(k_w, v_ext)
        return nd, floor

    def output_stage(h, nd, floor):
        num, den = nd[:, :dv], nd[:, dv:dv + 1]
        hh = num / jnp.maximum(jnp.abs(den), floor)
        hh = _rmsnorm(hh, hn_ref[:, h * dv:(h + 1) * dv])
        gate_o = jax.nn.sigmoid(o_ref[0, :, h * dv:(h + 1) * dv].astype(F32))
        y_ref[0, :, h * dv:(h + 1) * dv] = (hh * gate_o).astype(BF16)

    gated, valued = None, None
    for h in range(heads + 2):
        nxt_gated = gate_stage(h) if h < heads else None
        nxt_valued = value_stage(h - 1, *gated) if gated is not None else None
        if valued is not None:
            output_stage(h - 2, *valued)
        gated, valued = nxt_gated, nxt_valued


def _mlstm_core(p, gates, bias, hn, heads):
    bsz, s, _ = p.shape
    nq, nv = heads * MLSTM_QK, heads * MLSTM_V
    L = min(MLSTM_CHUNK, s)
    assert s % L == 0 and nv == 2 * nq
    return pl.pallas_call(
        functools.partial(_mlstm_body, heads=heads),
        grid=(bsz, s // L),
        in_specs=[
            pl.BlockSpec((1, L, nq), lambda b, c: (b, c, 0)),
            pl.BlockSpec((1, L, nq), lambda b, c: (b, c, 1)),
            pl.BlockSpec((1, L, nv), lambda b, c: (b, c, 1)),
            pl.BlockSpec((1, L, nv), lambda b, c: (b, c, 2)),
            pl.BlockSpec((1, L, 2 * LANES), lambda b, c: (b, c, 0)),
            pl.BlockSpec((1, 2 * LANES), lambda b, c: (0, 0)),
            pl.BlockSpec((1, nv), lambda b, c: (0, 0)),
        ],
        out_specs=pl.BlockSpec((1, L, nv), lambda b, c: (b, c, 0)),
        out_shape=jax.ShapeDtypeStruct((bsz, s, nv), BF16),
        scratch_shapes=[pltpu.VMEM((heads, MLSTM_QK, MLSTM_V + LANES), F32), pltpu.VMEM((1, LANES), F32)],
        compiler_params=_params("parallel", "arbitrary"),
        name="mlstm_core",
    )(p, p, p, p, gates, bias, hn)


FOX_BIAS_PIECES = 3


def _fox_gate_body(gate_ref, bias_ref, cp_ref, carry_ref, *, heads):
    @pl.when(pl.program_id(1) == 0)
    def _():
        carry_ref[...] = jnp.zeros_like(carry_ref)

    log_f = jax.nn.log_sigmoid(gate_ref[0] + bias_ref[...])
    cum = _cumsum_rows(log_f) + carry_ref[...]
    tc = cum.shape[0]
    carry_ref[...] = cum[tc - 1:tc, :]
    lane = lax.broadcasted_iota(jnp.int32, (tc, LANES), 1)
    neg_log2e = -1.4426950408889634
    for h in range(heads):
        c = jnp.broadcast_to(cum[:, h:h + 1] * neg_log2e, (tc, LANES))
        hi, mid, lo = (x.astype(F32) for x in _split3(c))
        pieces = jnp.where(lane == 0, hi, jnp.where(lane == 1, mid, jnp.where(lane == 2, lo, 0.0)))
        cp_ref[0, h] = pieces.astype(BF16)


def _fox_gates(gates, bias, heads):
    bsz, s, _ = gates.shape
    tc = min(FOX_GATE_CHUNK, s)
    assert s % tc == 0
    return pl.pallas_call(
        functools.partial(_fox_gate_body, heads=heads),
        grid=(bsz, s // tc),
        in_specs=[
            pl.BlockSpec((1, tc, LANES), lambda b, c: (b, c, 0)),
            pl.BlockSpec((1, LANES), lambda b, c: (0, 0)),
        ],
        out_specs=pl.BlockSpec((1, heads, tc, LANES), lambda b, c: (b, 0, c, 0)),
        out_shape=jax.ShapeDtypeStruct((bsz, heads, s, LANES), BF16),
        scratch_shapes=[pltpu.VMEM((1, LANES), F32)],
        compiler_params=_params("parallel", "arbitrary"),
        name="fox_gates",
    )(gates, bias)


def _fox_attn_body(q_ref, k_ref, v_ref, cp_ref, y_ref, sa_ref, sb_ref, m_ref, l_ref, acc_ref):
    tq = q_ref.shape[1]
    hps = q_ref.shape[2] // FOX_HEAD
    qi = pl.program_id(2)
    q_scale = (FOX_HEAD ** -0.5) * 1.4426950408889634
    lane = lax.broadcasted_iota(jnp.int32, (tq, LANES), 1)
    ones_cols = jnp.where(lane < FOX_BIAS_PIECES, 1.0, 0.0).astype(BF16)
    heads = [pl.ds(g * FOX_HEAD, FOX_HEAD) for g in range(hps)]
    q_ext = [jnp.concatenate([(q_ref[0, :, cols].astype(F32) * q_scale).astype(BF16), ones_cols], axis=1)
             for cols in heads]
    m_ref[...] = jnp.full_like(m_ref, NEG_BIG)
    l_ref[...] = jnp.zeros_like(l_ref)
    acc_ref[...] = jnp.zeros_like(acc_ref)

    def score_head(g, j, dst_ref, masked):
        rows = pl.ds(pl.multiple_of(j * tq, tq), tq)
        k_ext = jnp.concatenate([k_ref[0, rows, heads[g]], cp_ref[0, g, rows, :]], axis=1)
        s = _dot_nt(k_ext, q_ext[g])
        if masked:
            key = lax.broadcasted_iota(jnp.int32, (tq, tq), 0)
            qry = lax.broadcasted_iota(jnp.int32, (tq, tq), 1)
            s = jnp.where(key <= qry, s, NEG_BIG)
        dst_ref[g] = s

    def consume_head(g, j, src_ref):
        rows = pl.ds(pl.multiple_of(j * tq, tq), tq)
        m_old = m_ref[g]
        m_new = jnp.maximum(m_old, jnp.max(src_ref[g], axis=0, keepdims=True))
        alpha = jnp.exp2(m_old - m_new)
        p = jnp.exp2(src_ref[g] - m_new)
        l_ref[g] = alpha * l_ref[g] + jnp.sum(p, axis=0, keepdims=True)
        acc_ref[g] = alpha * acc_ref[g] + _dot_tn(v_ref[0, rows, heads[g]], p.astype(BF16))
        m_ref[g] = m_new

    def scores(j, dst_ref, masked):
        for g in range(hps):
            score_head(g, j, dst_ref, masked)

    def consume(j, src_ref):
        for g in range(hps):
            consume_head(g, j, src_ref)

    def advance(j_next, dst_ref, masked, j_cur, src_ref):
        for g in range(hps):
            score_head(g, j_next, dst_ref, masked)
            consume_head(g, j_cur, src_ref)

    one = jnp.int32(1)
    pl.when(qi == 0)(lambda: scores(0, sa_ref, True))
    pl.when(qi > 0)(lambda: scores(0, sa_ref, False))
    n_pairs = lax.shift_right_logical(jnp.maximum(qi - 1, 0), one)

    def pair_body(i, carry):
        j = 2 * i
        advance(j + 1, sb_ref, False, j, sa_ref)
        advance(j + 2, sa_ref, False, j + 1, sb_ref)
        return carry

    lax.fori_loop(0, n_pairs, pair_body, 0)

    @pl.when(qi == 0)
    def _():
        consume(0, sa_ref)

    @pl.when(lax.bitwise_and(qi, one) == 1)
    def _():
        advance(qi, sb_ref, True, qi - 1, sa_ref)
        consume(qi, sb_ref)

    @pl.when(jnp.logical_and(qi > 0, lax.bitwise_and(qi, one) == 0))
    def _():
        advance(qi - 1, sb_ref, False, qi - 2, sa_ref)
        advance(qi, sa_ref, True, qi - 1, sb_ref)
        consume(qi, sa_ref)

    for g, cols in enumerate(heads):
        y_ref[0, :, cols] = (acc_ref[g] / l_ref[g]).T.astype(BF16)


def _fox_attn(p, cp, heads):
    bsz, s, _ = p.shape
    tq = min(FOX_TQ, s)
    hps = FOX_HEADS_PER_STEP
    assert s % tq == 0 and heads % hps == 0
    hg, w = heads // hps, hps * FOX_HEAD
    return pl.pallas_call(
        _fox_attn_body,
        grid=(bsz, hg, s // tq),
        in_specs=[
            pl.BlockSpec((1, tq, w), lambda b, h, i: (b, i, h)),
            pl.BlockSpec((1, s, w), lambda b, h, i: (b, 0, hg + h)),
            pl.BlockSpec((1, s, w), lambda b, h, i: (b, 0, 2 * hg + h)),
            pl.BlockSpec((1, hps, s, LANES), lambda b, h, i: (b, h, 0, 0)),
        ],
        out_specs=pl.BlockSpec((1, tq, w), lambda b, h, i: (b, i, h)),
        out_shape=jax.ShapeDtypeStruct((bsz, s, heads * FOX_HEAD), BF16),
        scratch_shapes=[pltpu.VMEM((hps, tq, tq), F32), pltpu.VMEM((hps, tq, tq), F32),
                        pltpu.VMEM((hps, 1, tq), F32), pltpu.VMEM((hps, 1, tq), F32),
                        pltpu.VMEM((hps, FOX_HEAD, tq), F32)],
        compiler_params=_params("parallel", "parallel", "arbitrary"),
        name="fox_attn",
    )(p, p, p, cp)


def _gla_intra_t(q, k, b):
    L = q.shape[0]
    sub = min(GLA_SUB, L)

    a_t = jnp.zeros((L, L), F32)
    if L > sub:
        dk = q.shape[1]
        zeros = lambda n: [jnp.zeros((n, dk), BF16)] if n else []
        ks, qs = [], []
        for n in range(sub, L, sub):
            r = b[n - 1:n, :]
            k_part = (k[:n, :] * jnp.exp(r - b[:n, :])).astype(BF16)
            q_part = (q[n:n + sub, :] * jnp.exp(b[n:n + sub, :] - r)).astype(BF16)
            ks.append(jnp.concatenate([k_part] + zeros(L - n), axis=0))
            qs.append(jnp.concatenate(zeros(n) + [q_part] + zeros(L - n - sub), axis=0))
        a_t = _dot_nt(jnp.concatenate(ks, axis=1), jnp.concatenate(qs, axis=1))

    strip_t = lax.broadcasted_iota(jnp.int32, (sub, L), 1)
    strips = []
    for lo in range(0, L, sub):
        strip = jnp.zeros((sub, L), F32)
        for t in range(lo, lo + sub):
            hi = 8 * (t // 8 + 1)
            s_idx = lo + lax.broadcasted_iota(jnp.int32, (hi - lo, 1), 0)
            diff = jnp.where(s_idx <= t, b[t:t + 1, :] - b[lo:hi, :], NEG_BIG)
            w = jnp.exp(diff) * (k[lo:hi, :] * q[t:t + 1, :])
            colsum = jnp.sum(w, axis=1, keepdims=True)
            if hi < lo + sub:
                colsum = jnp.concatenate([colsum, jnp.zeros((lo + sub - hi, 1), F32)], axis=0)
            strip = jnp.where(strip_t == t, colsum, strip)
        strips.append(strip)
    return a_t + jnp.concatenate(strips, axis=0)


def _gla_body(q_ref, k_ref, v_ref, r_ref, alow_ref, wal_ref, bal_ref, hn_ref, y_ref, st_ref, *, heads):
    L = q_ref.shape[1]
    dk, dv = GLA_K, GLA_V
    qscale = dk ** -0.5

    @pl.when(pl.program_id(1) == 0)
    def _():
        st_ref[...] = jnp.zeros_like(st_ref)

    log_alpha = jax.nn.log_sigmoid(_dot_f32(alow_ref[0], wal_ref[...]) + bal_ref[...]) * (1.0 / GLA_GATE_TEMP)
    b_all = _cumsum_rows(log_alpha)

    def decay_stage(h):
        b = b_all[:, h * dk:(h + 1) * dk]
        q = q_ref[0, :, h * dk:(h + 1) * dk].astype(F32) * qscale
        k = k_ref[0, :, h * dk:(h + 1) * dk].astype(F32)
        a_t = _gla_intra_t(q, k, b).astype(BF16)
        b_last = b[L - 1:L, :]
        q_dec = (q * jnp.exp(b)).astype(BF16)
        k_dec = (k * jnp.exp(b_last - b)).astype(BF16)
        return a_t, q_dec, k_dec, jnp.exp(b_last)

    def value_stage(h, a_t, q_dec, k_dec, chunk_decay):
        v = v_ref[0, :, h * dv:(h + 1) * dv]
        st = st_ref[h]
        o = _dot_tn(a_t, v) + _dot_nt(q_dec, st.astype(BF16))
        st_ref[h] = chunk_decay * st + _dot_tn(v, k_dec)
        return (o,)

    def output_stage(h, o):
        o = _rmsnorm(o, hn_ref[:, h * dv:(h + 1) * dv])
        r = r_ref[0, :, h * dv:(h + 1) * dv].astype(F32)
        y_ref[0, :, h * dv:(h + 1) * dv] = (o * (r * jax.nn.sigmoid(r))).astype(BF16)

    decayed, valued = None, None
    for h in range(heads + 2):
        nxt_decayed = decay_stage(h) if h < heads else None
        nxt_valued = value_stage(h - 1, *decayed) if decayed is not None else None
        if valued is not None:
            output_stage(h - 2, *valued)
        decayed, valued = nxt_decayed, nxt_valued


def _gla_core(p, alow, w_alpha, b_alpha, hn, heads):
    bsz, s, _ = p.shape
    nk, nv = heads * GLA_K, heads * GLA_V
    L = min(GLA_CHUNK, s)
    assert s % L == 0 and nv == 2 * nk
    return pl.pallas_call(
        functools.partial(_gla_body, heads=heads),
        grid=(bsz, s // L),
        in_specs=[
            pl.BlockSpec((1, L, nk), lambda b, c: (b, c, 0)),
            pl.BlockSpec((1, L, nk), lambda b, c: (b, c, 1)),
            pl.BlockSpec((1, L, nv), lambda b, c: (b, c, 1)),
            pl.BlockSpec((1, L, nv), lambda b, c: (b, c, 2)),
            pl.BlockSpec((1, L, LANES), lambda b, c: (b, c, 0)),
            pl.BlockSpec((LANES, nk), lambda b, c: (0, 0)),
            pl.BlockSpec((1, nk), lambda b, c: (0, 0)),
            pl.BlockSpec((1, nv), lambda b, c: (0, 0)),
        ],
        out_specs=pl.BlockSpec((1, L, nv), lambda b, c: (b, c, 0)),
        out_shape=jax.ShapeDtypeStruct((bsz, s, nv), BF16),
        scratch_shapes=[pltpu.VMEM((heads, GLA_V, GLA_K), F32)],
        compiler_params=_params("parallel", "arbitrary"),
        name="gla_core",
    )(p, p, p, p, alow, w_alpha, b_alpha, hn)


def _pad_cols(w, width, offsets):
    out = jnp.zeros((w.shape[0], width), w.dtype)
    c0 = 0
    for off, n in offsets:
        out = out.at[:, off:off + n].set(w[:, c0:c0 + n])
        c0 += n
    return out


def _mlstm_layer(h, bsz, norm_g, w_in, w_in_bf, j, b_if, head_norm, w_out_bf):
    t, d = h.shape
    heads = d // MLSTM_V
    n_main = 2 * heads * MLSTM_QK + 2 * heads * MLSTM_V
    wg = _pad_cols(w_in[:, n_main:], 2 * LANES, [(0, heads), (LANES, heads)])
    bias = _pad_cols(b_if.reshape(1, -1), 2 * LANES, [(0, heads), (LANES, heads)])
    p, gates = _proj(h, norm_g, w_in_bf, j, n_main, wg.astype(BF16))
    y = _mlstm_core(p.reshape(bsz, t // bsz, n_main), gates.reshape(bsz, t // bsz, 2 * LANES), bias,
                    head_norm.reshape(1, -1), heads)
    return _out_proj(h, y.reshape(t, -1), w_out_bf, j)


def _fox_layer(h, bsz, norm_g, w_in, w_in_bf, j, b_f, w_out_bf):
    t, d = h.shape
    heads = d // FOX_HEAD
    n_main = 3 * heads * FOX_HEAD
    wg = _pad_cols(w_in[:, n_main:], LANES, [(0, heads)])
    bias = _pad_cols(b_f.reshape(1, -1), LANES, [(0, heads)])
    p, gates = _proj(h, norm_g, w_in_bf, j, n_main, wg.astype(BF16))
    s = t // bsz
    cp = _fox_gates(gates.reshape(bsz, s, LANES), bias, heads)
    y = _fox_attn(p.reshape(bsz, s, n_main), cp, heads)
    return _out_proj(h, y.reshape(t, -1), w_out_bf, j)


def _gla_layer(h, bsz, norm_g, w_in, w_in_bf, j, w_alpha, b_alpha, head_norm, w_out_bf):
    t, d = h.shape
    heads = d // GLA_V
    n_main = 2 * heads * GLA_K + 2 * heads * GLA_V
    rank = w_alpha.shape[0]
    wg = _pad_cols(w_in[:, n_main:], LANES, [(0, rank)])
    wal = jnp.zeros((LANES, w_alpha.shape[1]), F32).at[:rank].set(w_alpha)
    p, alow = _proj(h, norm_g, w_in_bf, j, n_main, wg.astype(BF16))
    s = t // bsz
    y = _gla_core(p.reshape(bsz, s, n_main), alow.reshape(bsz, s, LANES), wal, b_alpha.reshape(1, -1),
                  head_norm.reshape(1, -1), heads)
    return _out_proj(h, y.reshape(t, -1), w_out_bf, j)


def kernel(x, ffn_norm, ffn_w_in, ffn_w_out, mix_norm, a_w_in, a_b_if, a_head_norm, a_w_out, b_w_in, b_b_f, b_w_out, c_w_in, c_w_alpha, c_b_alpha, c_head_norm, c_w_out, final_norm):
    bsz, s, d = x.shape
    depth = ffn_norm.shape[0]
    h = x.reshape(bsz * s, d)
    ffn_in_bf, ffn_out_bf = ffn_w_in, ffn_w_out.astype(BF16)
    a_in_bf, a_out_bf = a_w_in.astype(BF16), a_w_out.astype(BF16)
    b_in_bf, b_out_bf = b_w_in.astype(BF16), b_w_out.astype(BF16)
    c_in_bf, c_out_bf = c_w_in.astype(BF16), c_w_out.astype(BF16)
    for i in range(depth):
        h = _ffn(h, ffn_norm[i, 0], ffn_in_bf, ffn_out_bf, (i, 0))
        kind, j = i % 3, i // 3
        if kind == 0:
            h = _mlstm_layer(h, bsz, mix_norm[i], a_w_in[j], a_in_bf, j, a_b_if[j], a_head_norm[j], a_out_bf)
        elif kind == 1:
            h = _fox_layer(h, bsz, mix_norm[i], b_w_in[j], b_in_bf, j, b_b_f[j], b_out_bf)
        else:
            h = _gla_layer(h, bsz, mix_norm[i], c_w_in[j], c_in_bf, j, c_w_alpha[j], c_b_alpha[j], c_head_norm[j],
                           c_out_bf)
        h = _ffn(h, ffn_norm[i, 1], ffn_in_bf, ffn_out_bf, (i, 1),
                 final_g=final_norm if i == depth - 1 else None)
    return h.reshape(bsz, s, d)
```
